```python
import jax, jax.numpy as jnp
from jax import lax
import numpy as np

D_MODEL = 1024
BATCH = 2
SEQ = 8192
DEPTH = 4
DEC_BATCH = 16
DEC_SEQ = 4096
PAST_LEN = 128

POOL_WIDTH = D_MODEL // 2
POOL_GROUPS = 4
POOL_GC = POOL_WIDTH // POOL_GROUPS
POOL_WINDOWS = (2, 4, 8, 16)
SGU_WIDTH = D_MODEL // 2
SGU_GROUPS = 4
SGU_GC = SGU_WIDTH // SGU_GROUPS
SGU_CHUNK = 128
HEAD_DIM = 64
ATTN_GROUPS = ((128, 1), (512, 4), (2048, 16))
HEADS_PER_GROUP = D_MODEL // 256
N_ATTN_HEADS = HEADS_PER_GROUP * len(ATTN_GROUPS)
ATTN_WIDTH = N_ATTN_HEADS * HEAD_DIM
ATTN_OUT = HEADS_PER_GROUP * HEAD_DIM
N_BUCKETS = 32
T5_MAX_DIST = 1024
N_BRANCH = 3
EPS = 1e-6
NEG_INF = -1e30
IN_SIZES = (POOL_WIDTH, POOL_WIDTH,
            SGU_WIDTH, SGU_WIDTH, SGU_WIDTH,
            ATTN_WIDTH, ATTN_WIDTH, ATTN_WIDTH,
            ATTN_OUT,
            N_BRANCH * D_MODEL)
N_IN = sum(IN_SIZES)

kernel_name = "hybrid_pool_sgu_dilated_encoder"


def _rmsnorm(x, g):
    xf = x.astype(jnp.float32)
    y = xf * lax.rsqrt(jnp.mean(xf * xf, axis=-1, keepdims=True) + EPS)
    return (y * g.astype(jnp.float32)).astype(x.dtype)


def _layernorm(x, g, b):
    xf = x.astype(jnp.float32)
    mu = jnp.mean(xf, axis=-1, keepdims=True)
    xc = xf - mu
    var = jnp.mean(xc * xc, axis=-1, keepdims=True)
    y = xc * lax.rsqrt(var + EPS) * g.astype(jnp.float32) + b.astype(jnp.float32)
    return y.astype(x.dtype)


def _t5_bucket(rel):
    half = N_BUCKETS // 2
    n = -rel
    ret = (n < 0).astype(np.int32) * half
    n = np.abs(n)
    max_exact = half // 2
    large = max_exact + (np.log(np.maximum(n, 1) / max_exact) / np.log(T5_MAX_DIST / max_exact)
                         * (half - max_exact)).astype(np.int32)
    large = np.minimum(large, half - 1)
    return (ret + np.where(n < max_exact, n, large)).astype(np.int32)


def _pool_mixer(xa, pool_w, pool_scale):
    B, S, _ = xa.shape
    xf = xa.reshape(B, S, POOL_GROUPS, POOL_GC).astype(jnp.float32)
    cs = jnp.concatenate([jnp.zeros((B, 1, POOL_GROUPS, POOL_GC), jnp.float32),
                          jnp.cumsum(xf, axis=1)], axis=1)
    t = jnp.arange(S, dtype=jnp.int32)
    pooled = []
    for gi, w in enumerate(POOL_WINDOWS):
        lo = jnp.clip(t - w // 2, 0, S - 1)
        hi = jnp.clip(t + w // 2 - 1, 0, S - 1)
        csg = cs[:, :, gi]
        cnt = (hi - lo + 1).astype(jnp.float32)
        pooled.append((csg[:, hi + 1] - csg[:, lo]) / cnt[None, :, None])
    mixed = (jnp.stack(pooled, axis=2) - xf).astype(xa.dtype)
    y = jnp.einsum('bsgc,gcd->bsgd', mixed, pool_w)
    return y.reshape(B, S, POOL_WIDTH) * pool_scale


def _sgu_mixer(u, v, ln_g, ln_b, w_s, b_s):
    B, S, _ = v.shape
    vn = _layernorm(v, ln_g, ln_b).reshape(B, S // SGU_CHUNK, SGU_CHUNK, SGU_GROUPS, SGU_GC)
    sp = jnp.einsum('gpq,bnqgc->bnpgc', w_s, vn) + jnp.transpose(b_s)[:, :, None]
    return u * sp.reshape(B, S, SGU_WIDTH)


def _dilated_group(q, k, v, table_g, window, dil):
    B, S, H, Dh = q.shape
    half = window // (2 * dil)
    blk = half
    L = S // dil
    Lp = -(-L // blk) * blk
    nb = Lp // blk
    N = B * dil

    def to_sub(t):
        return t.reshape(B, L, dil, H, Dh).transpose(0, 2, 1, 3, 4).reshape(N, L, H, Dh)

    def band(t):
        tp = jnp.pad(t, ((0, 0), (blk, Lp - L + blk), (0, 0), (0, 0))).reshape(N, nb + 2, blk, H, Dh)
        return jnp.concatenate([tp[:, :-2], tp[:, 1:-1], tp[:, 2:]], axis=2)

    qb = jnp.pad(to_sub(q), ((0, 0), (0, Lp - L), (0, 0), (0, 0))).reshape(N, nb, blk, H, Dh)
    kw = band(to_sub(k))
    vw = band(to_sub(v))
    rel = np.arange(3 * blk)[None, :] - blk - np.arange(blk)[:, None]
    bias = jnp.transpose(table_g[_t5_bucket(rel * dil)], (2, 0, 1)).astype(jnp.float32)
    keypos = np.arange(nb)[:, None] * blk - blk + np.arange(3 * blk)[None, :]
    valid = (np.abs(rel) <= half)[None] & ((keypos >= 0) & (keypos < L))[:, None, :]
    s = jnp.einsum('nbqhd,nbkhd->nbhqk', qb, kw).astype(jnp.float32) * (Dh ** -0.5) + bias
    s = jnp.where(valid[None, :, None], s, NEG_INF)
    m = jnp.max(s, axis=-1, keepdims=True)
    p = jnp.exp(s - m)
    den = jnp.sum(p, axis=-1)
    o = jnp.einsum('nbhqk,nbkhd->nbqhd', p.astype(vw.dtype), vw).astype(jnp.float32)
    o = o / jnp.swapaxes(den, 2, 3)[..., None]
    lse = jnp.swapaxes(m[..., 0] + jnp.log(den), 2, 3)
    o = o.reshape(N, Lp, H, Dh)[:, :L].reshape(B, dil, L, H, Dh).transpose(0, 2, 1, 3, 4).reshape(B, S, H, Dh)
    lse = lse.reshape(N, Lp, H)[:, :L].reshape(B, dil, L, H).transpose(0, 2, 1, 3).reshape(B, S, H)
    return o, lse


def _dilated_mixer(q, k, v, rel_bias):
    B, S, _ = q.shape
    q = q.reshape(B, S, N_ATTN_HEADS, HEAD_DIM)
    k = k.reshape(B, S, N_ATTN_HEADS, HEAD_DIM)
    v = v.reshape(B, S, N_ATTN_HEADS, HEAD_DIM)
    outs, lses = [], []
    for gi, (window, dil) in enumerate(ATTN_GROUPS):
        sl = slice(gi * HEADS_PER_GROUP, (gi + 1) * HEADS_PER_GROUP)
        o, l = _dilated_group(q[:, :, sl], k[:, :, sl], v[:, :, sl], rel_bias[:, sl], window, dil)
        outs.append(o)
        lses.append(l)
    wts = jax.nn.softmax(jnp.stack(lses, axis=0), axis=0)
    out = jnp.sum(wts[..., None] * jnp.stack(outs, axis=0), axis=0)
    return out.reshape(B, S, ATTN_OUT).astype(q.dtype)


def _encoder(x, norm_g, w_in, pool_w, pool_scale, sgu_ln_g, sgu_ln_b, sgu_w, sgu_b,
             rel_bias, w_br_a, w_br_b, w_br_c, w_out, final_g):
    split_at = [int(c) for c in np.cumsum(IN_SIZES)[:-1]]
    for l in range(DEPTH):
        h = _rmsnorm(x, norm_g[l])
        z = jnp.einsum('bsd,de->bse', h, w_in[l])
        (xa, ga, u, vv, gb, q, k, v, gc, mg) = jnp.split(z, split_at, axis=-1)
        a_out = _pool_mixer(xa, pool_w[l], pool_scale[l]) * jax.nn.silu(ga)
        b_out = _sgu_mixer(u, vv, sgu_ln_g[l], sgu_ln_b[l], sgu_w[l], sgu_b[l]) * jax.nn.silu(gb)
        c_out = _dilated_mixer(q, k, v, rel_bias) * jax.nn.silu(gc)
        gate_a, gate_b, gate_c = jnp.split(jax.nn.sigmoid(mg), N_BRANCH, axis=-1)
        merged = (gate_a * jnp.einsum('bsc,cd->bsd', a_out, w_br_a[l])
                  + gate_b * jnp.einsum('bsc,cd->bsd', b_out, w_br_b[l])
                  + gate_c * jnp.einsum('bsc,cd->bsd', c_out, w_br_c[l]))
        x = x + jnp.einsum('bsd,de->bse', merged, w_out[l])
    return _rmsnorm(x, final_g)


def setup_inputs(seed: int = 0) -> dict:
    key = jax.random.key(seed)
    ks = jax.random.split(key, 16)

    def nrm(k, shape, scale):
        return jax.random.normal(k, shape, jnp.float32) * scale

    return {
        'x_prompt': nrm(ks[0], (BATCH, SEQ, D_MODEL), 1.0),
        'x_sample': nrm(ks[1], (DEC_BATCH, DEC_SEQ, D_MODEL), 1.0),
        'norm_g': 1.0 + nrm(ks[2], (DEPTH, D_MODEL), 0.1),
        'w_in': nrm(ks[3], (DEPTH, D_MODEL, N_IN), D_MODEL ** -0.5),
        'pool_w': nrm(ks[4], (DEPTH, POOL_GROUPS, POOL_GC, POOL_GC), POOL_GC ** -0.5),
        'pool_scale': 1.0 + nrm(ks[5], (DEPTH, POOL_WIDTH), 0.1),
        'sgu_ln_g': 1.0 + nrm(ks[6], (DEPTH, SGU_WIDTH), 0.1),
        'sgu_ln_b': nrm(ks[7], (DEPTH, SGU_WIDTH), 0.1),
        'sgu_w': nrm(ks[8], (DEPTH, SGU_GROUPS, SGU_CHUNK, SGU_CHUNK), SGU_CHUNK ** -0.5),
        'sgu_b': 1.0 + nrm(ks[9], (DEPTH, SGU_GROUPS, SGU_CHUNK), 0.1),
        'rel_bias': nrm(ks[10], (N_BUCKETS, N_ATTN_HEADS), 0.5),
        'w_br_a': nrm(ks[11], (DEPTH, POOL_WIDTH, D_MODEL), POOL_WIDTH ** -0.5),
        'w_br_b': nrm(ks[12], (DEPTH, SGU_WIDTH, D_MODEL), SGU_WIDTH ** -0.5),
        'w_br_c': nrm(ks[13], (DEPTH, ATTN_OUT, D_MODEL), ATTN_OUT ** -0.5),
        'w_out': nrm(ks[14], (DEPTH, D_MODEL, D_MODEL), D_MODEL ** -0.5),
        'final_g': 1.0 + nrm(ks[15], (D_MODEL,), 0.1),
    }


def reference(x_prompt, x_sample, norm_g, w_in, pool_w, pool_scale, sgu_ln_g, sgu_ln_b, sgu_w,
              sgu_b, rel_bias, w_br_a, w_br_b, w_br_c, w_out, final_g):
    y_prompt = _encoder(x_prompt, norm_g, w_in, pool_w, pool_scale, sgu_ln_g, sgu_ln_b, sgu_w, sgu_b,
                        rel_bias, w_br_a, w_br_b, w_br_c, w_out, final_g)
    y_sample = _encoder(x_sample, norm_g, w_in, pool_w, pool_scale, sgu_ln_g, sgu_ln_b, sgu_w, sgu_b,
                        rel_bias, w_br_a, w_br_b, w_br_c, w_out, final_g)
    return (y_prompt, y_sample)
```

```python
import functools

import numpy as np
import jax
import jax.numpy as jnp
from jax import lax
from jax.experimental import pallas as pl
from jax.experimental.pallas import tpu as pltpu

F32 = jnp.float32
BF16 = jnp.bfloat16

D_MODEL = 1024
DEPTH = 4
POOL_WINDOWS = (2, 4, 8, 16)
POOL_HALO = 16
GROUP_WIDTH = 128
N_GROUPS = 4
SGU_CHUNK = 128
HEAD_DIM = 64
HEADS_PER_GROUP = 4
ATTN_OUT = HEADS_PER_GROUP * HEAD_DIM
ATTN_DILATIONS = (1, 4, 16)
ATTN_HALF = 64
N_BUCKETS = 32
T5_MAX_DIST = 1024
EPS = 1e-6
NEG_INF = -1e30

C_XA, C_GA, C_U, C_V, C_GB, C_QKV, C_GC, C_MG, C_END = 0, 512, 1024, 1536, 2048, 2560, 4864, 5120, 8192

ROW_TILE = 512
ATTN_TILE = 2048
Q_BLOCK = 128
K_BLOCK = Q_BLOCK + 2 * ATTN_HALF
VMEM_LIMIT_BYTES = 56 * 1024 * 1024


def _rmsnorm(x, g):
    ms = jnp.mean(x * x, axis=-1, keepdims=True)
    return x * lax.rsqrt(ms + EPS) * g


def _silu(x):
    return x * (1.0 / (1.0 + jnp.exp(-x)))


def _sigmoid(x):
    return 1.0 / (1.0 + jnp.exp(-x))


def _const_spec(shape):
    nd = len(shape)
    return pl.BlockSpec(shape, lambda *_: (0,) * nd, pipeline_mode=pl.Buffered(1))


def _proj_kernel(x_ref, g_ref, w_ref, *out_refs):
    h = _rmsnorm(x_ref[...], g_ref[...]).astype(BF16)
    z = jnp.dot(h, w_ref[...], preferred_element_type=F32)
    for idx, o_ref in enumerate(out_refs):
        blk = z[:, idx * ATTN_OUT:(idx + 1) * ATTN_OUT]
        if idx < len(ATTN_DILATIONS):
            blk = blk * (HEAD_DIM ** -0.5)
        o_ref[...] = blk.astype(BF16)


def _proj(x2d, g, w_qkv):
    T = x2d.shape[0]
    n_out = 3 * len(ATTN_DILATIONS)
    return pl.pallas_call(
        _proj_kernel,
        grid=(T // ROW_TILE,),
        in_specs=[pl.BlockSpec((ROW_TILE, D_MODEL), lambda i: (i, 0)),
                  _const_spec((1, D_MODEL)),
                  _const_spec(w_qkv.shape)],
        out_specs=[pl.BlockSpec((ROW_TILE, ATTN_OUT), lambda i: (i, 0))] * n_out,
        out_shape=[jax.ShapeDtypeStruct((T, ATTN_OUT), BF16)] * n_out,
        compiler_params=pltpu.CompilerParams(dimension_semantics=("arbitrary",),
                                             vmem_limit_bytes=VMEM_LIMIT_BYTES),
        name="proj",
    )(x2d, g, w_qkv)


def _attn_kernel(q_ref, kc_ref, kp_ref, kn_ref, vc_ref, vp_ref, vn_ref, bias_ref, o_ref, l_ref,
                 *, dil, rows, n_tiles):
    i = pl.program_id(1)
    nb = rows // Q_BLOCK
    lane = lax.broadcasted_iota(jnp.int32, (Q_BLOCK, ATTN_OUT), 1)
    head_masks = [(lane >= HEAD_DIM * h) & (lane < HEAD_DIM * (h + 1)) for h in range(HEADS_PER_GROUP)]
    col = lax.broadcasted_iota(jnp.int32, (HEADS_PER_GROUP * Q_BLOCK, K_BLOCK), 1)
    lo_cut = jnp.where(i == 0, ATTN_HALF, 0)
    hi_cut = jnp.where(i == n_tiles - 1, K_BLOCK - ATTN_HALF, K_BLOCK)

    def window(c_ref, p_ref, n_ref, j, cs):
        lo, hi = j * Q_BLOCK - ATTN_HALF, (j + 1) * Q_BLOCK + ATTN_HALF
        parts = []
        if lo < 0:
            parts.append(p_ref[0, :, cs])
            lo = 0
        parts.append(c_ref[0, lo:min(hi, rows), cs])
        if hi > rows:
            parts.append(n_ref[0, :, cs])
        return parts[0] if len(parts) == 1 else jnp.concatenate(parts, axis=0)

    for r in range(dil):
        cs = slice(r * ATTN_OUT, (r + 1) * ATTN_OUT)
        for j in range(nb):
            rs = slice(j * Q_BLOCK, (j + 1) * Q_BLOCK)
            qb = q_ref[0, rs, cs]
            kk = window(kc_ref, kp_ref, kn_ref, j, cs)
            vv = window(vc_ref, vp_ref, vn_ref, j, cs)
            zero = jnp.zeros_like(qb)
            qs = jnp.concatenate([jnp.where(head_masks[h], qb, zero) for h in range(HEADS_PER_GROUP)], axis=0)
            s = lax.dot_general(qs, kk, (((1,), (1,)), ((), ())), preferred_element_type=F32)
            s = s + bias_ref[...]
            if j == 0:
                s = jnp.where(col < lo_cut, NEG_INF, s)
            if j == nb - 1:
                s = jnp.where(col >= hi_cut, NEG_INF, s)
            m = jnp.max(s, axis=-1, keepdims=True)
            p = jnp.exp(s - m)
            den = jnp.sum(p, axis=-1, keepdims=True)
            o = jnp.dot(p.astype(BF16), vv, preferred_element_type=F32)
            o = o * (1.0 / den)
            lse = jnp.broadcast_to(m + jnp.log(den), o.shape)
            out = jnp.zeros((Q_BLOCK, ATTN_OUT), F32)
            lout = jnp.zeros((Q_BLOCK, ATTN_OUT), F32)
            for h in range(HEADS_PER_GROUP):
                hs = slice(h * Q_BLOCK, (h + 1) * Q_BLOCK)
                out = jnp.where(head_masks[h], o[hs], out)
                lout = jnp.where(head_masks[h], lse[hs], lout)
            o_ref[0, rs, cs] = out.astype(BF16)
            l_ref[0, rs, cs] = lout


def _attn(q, k, v, bias, B, S, dil):
    L = S // dil
    width = dil * ATTN_OUT
    rows = ATTN_TILE // dil
    n_tiles = L // rows
    halo_per_tile = rows // ATTN_HALF
    n_halo = L // ATTN_HALF
    qv, kv, vv = (t.reshape(B, L, width) for t in (q, k, v))
    cur = pl.BlockSpec((1, rows, width), lambda b, i: (b, i, 0))
    prev = pl.BlockSpec((1, ATTN_HALF, width), lambda b, i: (b, jnp.maximum(i * halo_per_tile - 1, 0), 0))
    nxt = pl.BlockSpec((1, ATTN_HALF, width),
                       lambda b, i: (b, jnp.minimum((i + 1) * halo_per_tile, n_halo - 1), 0))
    o, l = pl.pallas_call(
        functools.partial(_attn_kernel, dil=dil, rows=rows, n_tiles=n_tiles),
        grid=(B, n_tiles),
        in_specs=[cur, cur, prev, nxt, cur, prev, nxt, _const_spec(bias.shape)],
        out_specs=[cur, cur],
        out_shape=[jax.ShapeDtypeStruct((B, L, width), BF16), jax.ShapeDtypeStruct((B, L, width), F32)],
        compiler_params=pltpu.CompilerParams(dimension_semantics=("arbitrary", "arbitrary"),
                                             vmem_limit_bytes=VMEM_LIMIT_BYTES),
        name=f"attn_d{dil}",
    )(qv, kv, kv, kv, vv, vv, vv, bias)
    return o.reshape(B * S, ATTN_OUT), l.reshape(B * S, ATTN_OUT)


def _mix_kernel(x_ref, xp_ref, xn_ref, o0_ref, o1_ref, o2_ref, l0_ref, l1_ref, l2_ref,
                g_ref, w_in_ref, pool_w_ref, pool_scale_ref, ln_g_ref, ln_b_ref, sgu_w_ref, sgu_b_ref,
                w_a_ref, w_b_ref, w_c_ref, w_out_ref, fg_ref, y_ref, xa_scr,
                *, seq_len, final):
    TM = ROW_TILE
    tiles_per_seq = seq_len // TM
    pos0 = (pl.program_id(0) % tiles_per_seq) * TM
    g = g_ref[...]
    x = x_ref[...]
    h = _rmsnorm(x, g).astype(BF16)

    def zcols(a, b):
        return jnp.dot(h, w_in_ref[:, a:b], preferred_element_type=F32)

    hp = _rmsnorm(xp_ref[...], g).astype(BF16)
    hn = _rmsnorm(xn_ref[...], g).astype(BF16)
    h_ext = jnp.concatenate([hp, h, hn], axis=0)
    xa_ext = jnp.dot(h_ext, w_in_ref[:, C_XA:C_GA], preferred_element_type=F32)
    row = lax.broadcasted_iota(jnp.int32, xa_ext.shape, 0) + (pos0 - POOL_HALO)
    xa_ext = jnp.where((row >= 0) & (row < seq_len), xa_ext, 0.0)
    xa_scr[...] = xa_ext
    t = lax.broadcasted_iota(jnp.int32, (TM, GROUP_WIDTH), 0) + pos0
    a_parts = []
    for gi, w in enumerate(POOL_WINDOWS):
        gs = slice(gi * GROUP_WIDTH, (gi + 1) * GROUP_WIDTH)
        acc = xa_scr[pl.ds(POOL_HALO - w // 2, TM), gs]
        for off in range(-w // 2 + 1, w // 2):
            acc = acc + xa_scr[pl.ds(POOL_HALO + off, TM), gs]
        cnt = (jnp.minimum(t + (w // 2 - 1), seq_len - 1) - jnp.maximum(t - w // 2, 0) + 1).astype(F32)
        mixed = acc / cnt - xa_scr[pl.ds(POOL_HALO, TM), gs]
        a_parts.append(jnp.dot(mixed.astype(BF16), pool_w_ref[gi], preferred_element_type=F32))
    a_mix = jnp.concatenate(a_parts, axis=1) * pool_scale_ref[...]
    a_out = (a_mix * _silu(zcols(C_GA, C_U))).astype(BF16)
    merged = _sigmoid(zcols(C_MG, C_MG + D_MODEL)) * jnp.dot(a_out, w_a_ref[...], preferred_element_type=F32)

    u = zcols(C_U, C_V)
    v = zcols(C_V, C_GB)
    mu = jnp.mean(v, axis=-1, keepdims=True)
    vc = v - mu
    var = jnp.mean(vc * vc, axis=-1, keepdims=True)
    vn = (vc * lax.rsqrt(var + EPS) * ln_g_ref[...] + ln_b_ref[...]).astype(BF16)
    n_chunks = TM // SGU_CHUNK
    sp_cols = []
    for gi in range(N_GROUPS):
        gs = slice(gi * GROUP_WIDTH, (gi + 1) * GROUP_WIDTH)
        rhs = jnp.concatenate([vn[c * SGU_CHUNK:(c + 1) * SGU_CHUNK, gs] for c in range(n_chunks)], axis=1)
        sp_g = jnp.dot(sgu_w_ref[gi], rhs, preferred_element_type=F32)
        sp_cols.append(jnp.concatenate(
            [sp_g[:, c * GROUP_WIDTH:(c + 1) * GROUP_WIDTH] + sgu_b_ref[gi] for c in range(n_chunks)], axis=0))
    sp = jnp.concatenate(sp_cols, axis=1)
    b_out = (u * sp * _silu(zcols(C_GB, C_QKV))).astype(BF16)
    merged = merged + (_sigmoid(zcols(C_MG + D_MODEL, C_MG + 2 * D_MODEL))
                       * jnp.dot(b_out, w_b_ref[...], preferred_element_type=F32))

    l0, l1, l2 = l0_ref[...], l1_ref[...], l2_ref[...]
    lmax = jnp.maximum(jnp.maximum(l0, l1), l2)
    e0, e1, e2 = jnp.exp(l0 - lmax), jnp.exp(l1 - lmax), jnp.exp(l2 - lmax)
    c = (e0 * o0_ref[...].astype(F32) + e1 * o1_ref[...].astype(F32) + e2 * o2_ref[...].astype(F32)) / (e0 + e1 + e2)
    c_out = (c * _silu(zcols(C_GC, C_MG))).astype(BF16)
    merged = merged + (_sigmoid(zcols(C_MG + 2 * D_MODEL, C_END))
                       * jnp.dot(c_out, w_c_ref[...], preferred_element_type=F32))

    y = x + jnp.dot(merged.astype(BF16), w_out_ref[...], preferred_element_type=F32)
    if final:
        y = _rmsnorm(y, fg_ref[...])
    y_ref[...] = y


def _mix(x2d, seq_len, attn_o, attn_l, lw, final_g, final):
    T = x2d.shape[0]
    halo_per_tile = ROW_TILE // POOL_HALO
    n_halo = T // POOL_HALO
    row = lambda width: pl.BlockSpec((ROW_TILE, width), lambda i: (i, 0))
    prev = pl.BlockSpec((POOL_HALO, D_MODEL), lambda i: (jnp.maximum(i * halo_per_tile - 1, 0), 0))
    nxt = pl.BlockSpec((POOL_HALO, D_MODEL), lambda i: (jnp.minimum((i + 1) * halo_per_tile, n_halo - 1), 0))
    weights = [lw["norm_g"], lw["w_in"], lw["pool_w"], lw["pool_scale"], lw["ln_g"], lw["ln_b"],
               lw["sgu_w"], lw["sgu_b"], lw["w_a"], lw["w_b"], lw["w_c"], lw["w_out"], final_g]
    return pl.pallas_call(
        functools.partial(_mix_kernel, seq_len=seq_len, final=final),
        grid=(T // ROW_TILE,),
        in_specs=[row(D_MODEL), prev, nxt] + [row(ATTN_OUT)] * 6 + [_const_spec(w.shape) for w in weights],
        out_specs=row(D_MODEL),
        out_shape=jax.ShapeDtypeStruct((T, D_MODEL), F32),
        scratch_shapes=[pltpu.VMEM((ROW_TILE + 2 * POOL_HALO, 4 * GROUP_WIDTH), F32)],
        compiler_params=pltpu.CompilerParams(dimension_semantics=("arbitrary",),
                                             vmem_limit_bytes=VMEM_LIMIT_BYTES),
        name="mix_final" if final else "mix",
    )(x2d, x2d, x2d, *attn_o, *attn_l, *weights)


def _t5_bucket(rel):
    half = N_BUCKETS // 2
    n = -rel
    ret = (n < 0).astype(np.int32) * half
    n = np.abs(n)
    max_exact = half // 2
    large = max_exact + (np.log(np.maximum(n, 1) / max_exact) / np.log(T5_MAX_DIST / max_exact)
                         * (half - max_exact)).astype(np.int32)
    large = np.minimum(large, half - 1)
    return (ret + np.where(n < max_exact, n, large)).astype(np.int32)


def _attn_bias(rel_bias, gi, dil):
    rel = np.arange(K_BLOCK)[None, :] - ATTN_HALF - np.arange(Q_BLOCK)[:, None]
    table = rel_bias[:, gi * HEADS_PER_GROUP:(gi + 1) * HEADS_PER_GROUP].astype(F32)
    bias = jnp.transpose(table[_t5_bucket(rel * dil)], (2, 0, 1))
    bias = jnp.where((np.abs(rel) <= ATTN_HALF)[None], bias, NEG_INF)
    return bias.reshape(HEADS_PER_GROUP * Q_BLOCK, K_BLOCK)


def _layer_weights(l, norm_g, w_in, pool_w, pool_scale, sgu_ln_g, sgu_ln_b, sgu_w, sgu_b,
                   w_br_a, w_br_b, w_br_c, w_out):
    return {
        "norm_g": norm_g[l].reshape(1, D_MODEL),
        "w_in": w_in[l].astype(BF16),
        "w_qkv": w_in[l, :, C_QKV:C_GC].astype(BF16),
        "pool_w": pool_w[l].astype(BF16),
        "pool_scale": pool_scale[l].reshape(1, -1),
        "ln_g": sgu_ln_g[l].reshape(1, -1),
        "ln_b": sgu_ln_b[l].reshape(1, -1),
        "sgu_w": sgu_w[l].astype(BF16),
        "sgu_b": jnp.broadcast_to(sgu_b[l][:, :, None], (N_GROUPS, SGU_CHUNK, GROUP_WIDTH)),
        "w_a": w_br_a[l].astype(BF16),
        "w_b": w_br_b[l].astype(BF16),
        "w_c": w_br_c[l].astype(BF16),
        "w_out": w_out[l].astype(BF16),
    }


def _encoder(x, layers, biases, final_g):
    B, S, _ = x.shape
    assert S % ATTN_TILE == 0 and S % (ATTN_TILE // max(ATTN_DILATIONS)) == 0
    x2d = x.reshape(B * S, D_MODEL)
    for l, lw in enumerate(layers):
        qkv = _proj(x2d, lw["norm_g"], lw["w_qkv"])
        n = len(ATTN_DILATIONS)
        outs = [_attn(qkv[gi], qkv[n + gi], qkv[2 * n + gi], biases[gi], B, S, dil)
                for gi, dil in enumerate(ATTN_DILATIONS)]
        x2d = _mix(x2d, S, [o for o, _ in outs], [lse for _, lse in outs], lw, final_g,
                   final=(l == len(layers) - 1))
    return x2d.reshape(B, S, D_MODEL)


def kernel(x_prompt, x_sample, norm_g, w_in, pool_w, pool_scale, sgu_ln_g, sgu_ln_b, sgu_w, sgu_b,
           rel_bias, w_br_a, w_br_b, w_br_c, w_out, final_g):
    layers = [_layer_weights(l, norm_g, w_in, pool_w, pool_scale, sgu_ln_g, sgu_ln_b, sgu_w, sgu_b,
                             w_br_a, w_br_b, w_br_c, w_out) for l in range(DEPTH)]
    biases = [_attn_bias(rel_bias, gi, dil) for gi, dil in enumerate(ATTN_DILATIONS)]
    fg = final_g.reshape(1, D_MODEL)
    return (_encoder(x_prompt, layers, biases, fg), _encoder(x_sample, layers, biases, fg))
```

```python
import functools

import numpy as np
import jax
import jax.numpy as jnp
from jax import lax
from jax.experimental import pallas as pl
from jax.experimental.pallas import tpu as pltpu

F32 = jnp.float32
BF16 = jnp.bfloat16

D_MODEL = 1024
DEPTH = 4
POOL_WINDOWS = (2, 4, 8, 16)
POOL_HALO = 16
LANES = 128
GROUP_WIDTH = 128
N_GROUPS = 4
SGU_CHUNK = 128
HEAD_DIM = 64
HEADS_PER_GROUP = 4
ATTN_OUT = HEADS_PER_GROUP * HEAD_DIM
ATTN_DILATIONS = (1, 4, 16)
ATTN_HALF = 64
N_BUCKETS = 32
T5_MAX_DIST = 1024
EPS = 1e-6
NEG_INF = -1e30

C_XA, C_GA, C_U, C_V, C_GB, C_QKV, C_GC, C_MG, C_END = 0, 512, 1024, 1536, 2048, 2560, 4864, 5120, 8192

ROW_TILE = 512
ATTN_TILE = 2048
Q_BLOCK = 128
K_BLOCK = Q_BLOCK + 2 * ATTN_HALF
VMEM_LIMIT_BYTES = 56 * 1024 * 1024


def _rmsnorm(x, g):
    ms = jnp.mean(x * x, axis=-1, keepdims=True)
    return x * lax.rsqrt(ms + EPS) * g


def _silu(x):
    return x * (1.0 / (1.0 + jnp.exp(-x)))


def _sigmoid(x):
    return 1.0 / (1.0 + jnp.exp(-x))


def _const_spec(shape):
    nd = len(shape)
    return pl.BlockSpec(shape, lambda *_: (0,) * nd, pipeline_mode=pl.Buffered(1))


def _proj_kernel(x_ref, g_ref, w_ref, *refs):
    out_refs, scr = refs[:-1], refs[-1]
    n = len(ATTN_DILATIONS)
    h = _rmsnorm(x_ref[...], g_ref[...]).astype(BF16)
    z = jnp.dot(h, w_ref[...], preferred_element_type=F32)
    slab = 0
    for idx, o_ref in enumerate(out_refs):
        dil = ATTN_DILATIONS[idx % n]
        blk = z[:, idx * ATTN_OUT:(idx + 1) * ATTN_OUT]
        if idx < n:
            blk = blk * (HEAD_DIM ** -0.5)
        if dil == 1:
            o_ref[...] = blk.astype(BF16)
            continue
        rows = ROW_TILE // dil
        for half in range(ATTN_OUT // LANES):
            scr[slab] = blk[:, half * LANES:(half + 1) * LANES]
            for r in range(dil):
                c0 = r * ATTN_OUT + half * LANES
                o_ref[:, c0:c0 + LANES] = scr[slab, pl.ds(r, rows, stride=dil), :].astype(BF16)
            slab += 1


def _proj(x2d, g, w_qkv):
    T = x2d.shape[0]
    n = len(ATTN_DILATIONS)
    dils = [ATTN_DILATIONS[idx % n] for idx in range(3 * n)]
    n_slabs = sum(ATTN_OUT // LANES for d in dils if d > 1)
    return pl.pallas_call(
        _proj_kernel,
        grid=(T // ROW_TILE,),
        in_specs=[pl.BlockSpec((ROW_TILE, D_MODEL), lambda i: (i, 0)),
                  _const_spec((1, D_MODEL)),
                  _const_spec(w_qkv.shape)],
        out_specs=[pl.BlockSpec((ROW_TILE // d, d * ATTN_OUT), lambda i: (i, 0)) for d in dils],
        out_shape=[jax.ShapeDtypeStruct((T // d, d * ATTN_OUT), BF16) for d in dils],
        scratch_shapes=[pltpu.VMEM((n_slabs, ROW_TILE, LANES), F32)],
        compiler_params=pltpu.CompilerParams(dimension_semantics=("arbitrary",),
                                             vmem_limit_bytes=VMEM_LIMIT_BYTES),
        name="proj",
    )(x2d, g, w_qkv)


def _attn_kernel(q_ref, kc_ref, kp_ref, kn_ref, vc_ref, vp_ref, vn_ref, bias_ref, o_ref, l_ref,
                 *, dil, rows, n_tiles):
    i = pl.program_id(1)
    nb = rows // Q_BLOCK
    lane = lax.broadcasted_iota(jnp.int32, (Q_BLOCK, ATTN_OUT), 1)
    head_masks = [(lane >= HEAD_DIM * h) & (lane < HEAD_DIM * (h + 1)) for h in range(HEADS_PER_GROUP)]
    col = lax.broadcasted_iota(jnp.int32, (HEADS_PER_GROUP * Q_BLOCK, K_BLOCK), 1)
    lo_cut = jnp.where(i == 0, ATTN_HALF, 0)
    hi_cut = jnp.where(i == n_tiles - 1, K_BLOCK - ATTN_HALF, K_BLOCK)

    def window(c_ref, p_ref, n_ref, j, cs):
        lo, hi = j * Q_BLOCK - ATTN_HALF, (j + 1) * Q_BLOCK + ATTN_HALF
        parts = []
        if lo < 0:
            parts.append(p_ref[0, :, cs])
            lo = 0
        parts.append(c_ref[0, lo:min(hi, rows), cs])
        if hi > rows:
            parts.append(n_ref[0, :, cs])
        return parts[0] if len(parts) == 1 else jnp.concatenate(parts, axis=0)

    for r in range(dil):
        cs = slice(r * ATTN_OUT, (r + 1) * ATTN_OUT)
        for j in range(nb):
            rs = slice(j * Q_BLOCK, (j + 1) * Q_BLOCK)
            qb = q_ref[0, rs, cs]
            kk = window(kc_ref, kp_ref, kn_ref, j, cs)
            vv = window(vc_ref, vp_ref, vn_ref, j, cs)
            zero = jnp.zeros_like(qb)
            qs = jnp.concatenate([jnp.where(head_masks[h], qb, zero) for h in range(HEADS_PER_GROUP)], axis=0)
            s = lax.dot_general(qs, kk, (((1,), (1,)), ((), ())), preferred_element_type=F32)
            s = s + bias_ref[...]
            if j == 0:
                s = jnp.where(col < lo_cut, NEG_INF, s)
            if j == nb - 1:
                s = jnp.where(col >= hi_cut, NEG_INF, s)
            m = jnp.max(s, axis=-1, keepdims=True)
            p = jnp.exp(s - m)
            den = jnp.sum(p, axis=-1, keepdims=True)
            o = jnp.dot(p.astype(BF16), vv, preferred_element_type=F32)
            o = o * (1.0 / den)
            lse = jnp.broadcast_to(m + jnp.log(den), o.shape)
            out = jnp.zeros((Q_BLOCK, ATTN_OUT), F32)
            lout = jnp.zeros((Q_BLOCK, ATTN_OUT), F32)
            for h in range(HEADS_PER_GROUP):
                hs = slice(h * Q_BLOCK, (h + 1) * Q_BLOCK)
                out = jnp.where(head_masks[h], o[hs], out)
                lout = jnp.where(head_masks[h], lse[hs], lout)
            o_ref[0, rs, cs] = out.astype(BF16)
            l_ref[0, rs, cs] = lout


def _attn(q, k, v, bias, B, S, dil):
    L = S // dil
    width = dil * ATTN_OUT
    rows = ATTN_TILE // dil
    n_tiles = L // rows
    halo_per_tile = rows // ATTN_HALF
    n_halo = L // ATTN_HALF
    qv, kv, vv = (t.reshape(B, L, width) for t in (q, k, v))
    cur = pl.BlockSpec((1, rows, width), lambda b, i: (b, i, 0))
    prev = pl.BlockSpec((1, ATTN_HALF, width), lambda b, i: (b, jnp.maximum(i * halo_per_tile - 1, 0), 0))
    nxt = pl.BlockSpec((1, ATTN_HALF, width),
                       lambda b, i: (b, jnp.minimum((i + 1) * halo_per_tile, n_halo - 1), 0))
    o, l = pl.pallas_call(
        functools.partial(_attn_kernel, dil=dil, rows=rows, n_tiles=n_tiles),
        grid=(B, n_tiles),
        in_specs=[cur, cur, prev, nxt, cur, prev, nxt, _const_spec(bias.shape)],
        out_specs=[cur, cur],
        out_shape=[jax.ShapeDtypeStruct((B, L, width), BF16), jax.ShapeDtypeStruct((B, L, width), F32)],
        compiler_params=pltpu.CompilerParams(dimension_semantics=("arbitrary", "arbitrary"),
                                             vmem_limit_bytes=VMEM_LIMIT_BYTES),
        name=f"attn_d{dil}",
    )(qv, kv, kv, kv, vv, vv, vv, bias)
    return o.reshape(B * L, width), l.reshape(B * L, width)


def _mix_kernel(x_ref, xp_ref, xn_ref, o0_ref, o1_ref, o2_ref, l0_ref, l1_ref, l2_ref,
                g_ref, w_in_ref, pool_w_ref, pool_scale_ref, ln_g_ref, ln_b_ref, sgu_w_ref, sgu_b_ref,
                w_a_ref, w_b_ref, w_c_ref, w_out_ref, fg_ref, y_ref, xa_scr, il_scr,
                *, seq_len, final):
    TM = ROW_TILE
    tiles_per_seq = seq_len // TM
    pos0 = (pl.program_id(0) % tiles_per_seq) * TM
    g = g_ref[...]
    x = x_ref[...]
    h = _rmsnorm(x, g).astype(BF16)

    def zcols(a, b):
        return jnp.dot(h, w_in_ref[:, a:b], preferred_element_type=F32)

    hp = _rmsnorm(xp_ref[...], g).astype(BF16)
    hn = _rmsnorm(xn_ref[...], g).astype(BF16)
    h_ext = jnp.concatenate([hp, h, hn], axis=0)
    xa_ext = jnp.dot(h_ext, w_in_ref[:, C_XA:C_GA], preferred_element_type=F32)
    row = lax.broadcasted_iota(jnp.int32, xa_ext.shape, 0) + (pos0 - POOL_HALO)
    xa_ext = jnp.where((row >= 0) & (row < seq_len), xa_ext, 0.0)
    xa_scr[...] = xa_ext
    t = lax.broadcasted_iota(jnp.int32, (TM, GROUP_WIDTH), 0) + pos0
    a_parts = []
    for gi, w in enumerate(POOL_WINDOWS):
        gs = slice(gi * GROUP_WIDTH, (gi + 1) * GROUP_WIDTH)
        acc = xa_scr[pl.ds(POOL_HALO - w // 2, TM), gs]
        for off in range(-w // 2 + 1, w // 2):
            acc = acc + xa_scr[pl.ds(POOL_HALO + off, TM), gs]
        cnt = (jnp.minimum(t + (w // 2 - 1), seq_len - 1) - jnp.maximum(t - w // 2, 0) + 1).astype(F32)
        mixed = acc / cnt - xa_scr[pl.ds(POOL_HALO, TM), gs]
        a_parts.append(jnp.dot(mixed.astype(BF16), pool_w_ref[gi], preferred_element_type=F32))
    a_mix = jnp.concatenate(a_parts, axis=1) * pool_scale_ref[...]
    a_out = (a_mix * _silu(zcols(C_GA, C_U))).astype(BF16)
    merged = _sigmoid(zcols(C_MG, C_MG + D_MODEL)) * jnp.dot(a_out, w_a_ref[...], preferred_element_type=F32)

    u = zcols(C_U, C_V)
    v = zcols(C_V, C_GB)
    mu = jnp.mean(v, axis=-1, keepdims=True)
    vc = v - mu
    var = jnp.mean(vc * vc, axis=-1, keepdims=True)
    vn = (vc * lax.rsqrt(var + EPS) * ln_g_ref[...] + ln_b_ref[...]).astype(BF16)
    n_chunks = TM // SGU_CHUNK
    sp_cols = []
    for gi in range(N_GROUPS):
        gs = slice(gi * GROUP_WIDTH, (gi + 1) * GROUP_WIDTH)
        rhs = jnp.concatenate([vn[c * SGU_CHUNK:(c + 1) * SGU_CHUNK, gs] for c in range(n_chunks)], axis=1)
        sp_g = jnp.dot(sgu_w_ref[gi], rhs, preferred_element_type=F32)
        sp_cols.append(jnp.concatenate(
            [sp_g[:, c * GROUP_WIDTH:(c + 1) * GROUP_WIDTH] + sgu_b_ref[gi] for c in range(n_chunks)], axis=0))
    sp = jnp.concatenate(sp_cols, axis=1)
    b_out = (u * sp * _silu(zcols(C_GB, C_QKV))).astype(BF16)
    merged = merged + (_sigmoid(zcols(C_MG + D_MODEL, C_MG + 2 * D_MODEL))
                       * jnp.dot(b_out, w_b_ref[...], preferred_element_type=F32))

    def token_order(ref, gi, half, slab):
        dil = ATTN_DILATIONS[gi]
        if dil == 1:
            return ref[:, half * LANES:(half + 1) * LANES].astype(F32)
        for r in range(dil):
            c0 = r * ATTN_OUT + half * LANES
            il_scr[slab, pl.ds(r, TM // dil, stride=dil), :] = ref[:, c0:c0 + LANES].astype(F32)
        return il_scr[slab]

    c_halves = []
    slab = 0
    for half in range(ATTN_OUT // LANES):
        os_, ls_ = [], []
        for gi, (o_ref, l_ref) in enumerate(((o0_ref, l0_ref), (o1_ref, l1_ref), (o2_ref, l2_ref))):
            os_.append(token_order(o_ref, gi, half, slab))
            ls_.append(token_order(l_ref, gi, half, slab + 1))
            if ATTN_DILATIONS[gi] > 1:
                slab += 2
        lmax = jnp.maximum(jnp.maximum(ls_[0], ls_[1]), ls_[2])
        es = [jnp.exp(l - lmax) for l in ls_]
        c_halves.append((es[0] * os_[0] + es[1] * os_[1] + es[2] * os_[2]) / (es[0] + es[1] + es[2]))
    c = jnp.concatenate(c_halves, axis=1)
    c_out = (c * _silu(zcols(C_GC, C_MG))).astype(BF16)
    merged = merged + (_sigmoid(zcols(C_MG + 2 * D_MODEL, C_END))
                       * jnp.dot(c_out, w_c_ref[...], preferred_element_type=F32))

    y = x + jnp.dot(merged.astype(BF16), w_out_ref[...], preferred_element_type=F32)
    if final:
        y = _rmsnorm(y, fg_ref[...])
    y_ref[...] = y


def _mix(x2d, seq_len, attn_o, attn_l, lw, final_g, final):
    T = x2d.shape[0]
    halo_per_tile = ROW_TILE // POOL_HALO
    n_halo = T // POOL_HALO
    row = lambda width: pl.BlockSpec((ROW_TILE, width), lambda i: (i, 0))
    attn_spec = lambda d: pl.BlockSpec((ROW_TILE // d, d * ATTN_OUT), lambda i: (i, 0))
    n_il_slabs = 2 * (ATTN_OUT // LANES) * sum(d > 1 for d in ATTN_DILATIONS)
    prev =pl.BlockSpec((POOL_HALO, D_MODEL), lambda i: (jnp.maximum(i * halo_per_tile - 1, 0), 0))
    nxt = pl.BlockSpec((POOL_HALO, D_MODEL), lambda i: (jnp.minimum((i + 1) * halo_per_tile, n_halo - 1), 0))
    weights = [lw["norm_g"], lw["w_in"], lw["pool_w"], lw["pool_scale"], lw["ln_g"], lw["ln_b"],
               lw["sgu_w"], lw["sgu_b"], lw["w_a"], lw["w_b"], lw["w_c"], lw["w_out"], final_g]
    return pl.pallas_call(
        functools.partial(_mix_kernel, seq_len=seq_len, final=final),
        grid=(T // ROW_TILE,),
        in_specs=[row(D_MODEL), prev, nxt] + 2 * [attn_spec(d) for d in ATTN_DILATIONS]
        + [_const_spec(w.shape) for w in weights],
        out_specs=row(D_MODEL),
        out_shape=jax.ShapeDtypeStruct((T, D_MODEL), F32),
        scratch_shapes=[pltpu.VMEM((ROW_TILE + 2 * POOL_HALO, 4 * GROUP_WIDTH), F32),
                        pltpu.VMEM((n_il_slabs, ROW_TILE, LANES), F32)],
        compiler_params=pltpu.CompilerParams(dimension_semantics=("arbitrary",),
                                             vmem_limit_bytes=VMEM_LIMIT_BYTES),
        name="mix_final" if final else "mix",
    )(x2d, x2d, x2d, *attn_o, *attn_l, *weights)


def _t5_bucket(rel):
    half = N_BUCKETS // 2
    n = -rel
    ret = (n < 0).astype(np.int32) * half
    n = np.abs(n)
    max_exact = half // 2
    large = max_exact + (np.log(np.maximum(n, 1) / max_exact) / np.log(T5_MAX_DIST / max_exact)
                         * (half - max_exact)).astype(np.int32)
    large = np.minimum(large, half - 1)
    return (ret + np.where(n < max_exact, n, large)).astype(np.int32)


def _attn_bias(rel_bias, gi, dil):
    rel = np.arange(K_BLOCK)[None, :] - ATTN_HALF - np.arange(Q_BLOCK)[:, None]
    table = rel_bias[:, gi * HEADS_PER_GROUP:(gi + 1) * HEADS_PER_GROUP].astype(F32)
    onehot = (_t5_bucket(rel * dil)[..., None] == np.arange(N_BUCKETS)).astype(np.float32)
    bias = jnp.einsum("qkb,bh->hqk", onehot, table, precision=lax.Precision.HIGHEST)
    bias = jnp.where((np.abs(rel) <= ATTN_HALF)[None], bias, NEG_INF)
    return bias.reshape(HEADS_PER_GROUP * Q_BLOCK, K_BLOCK)


def _layer_weights(l, norm_g, w_in, pool_w, pool_scale, sgu_ln_g, sgu_ln_b, sgu_w, sgu_b,
                   w_br_a, w_br_b, w_br_c, w_out):
    return {
        "norm_g": norm_g[l].reshape(1, D_MODEL),
        "w_in": w_in[l].astype(BF16),
        "w_qkv": w_in[l, :, C_QKV:C_GC].astype(BF16),
        "pool_w": pool_w[l].astype(BF16),
        "pool_scale": pool_scale[l].reshape(1, -1),
        "ln_g": sgu_ln_g[l].reshape(1, -1),
        "ln_b": sgu_ln_b[l].reshape(1, -1),
        "sgu_w": sgu_w[l].astype(BF16),
        "sgu_b": jnp.broadcast_to(sgu_b[l][:, :, None], (N_GROUPS, SGU_CHUNK, GROUP_WIDTH)),
        "w_a": w_br_a[l].astype(BF16),
        "w_b": w_br_b[l].astype(BF16),
        "w_c": w_br_c[l].astype(BF16),
        "w_out": w_out[l].astype(BF16),
    }


def _encoder(x, layers, biases, final_g):
    B, S, _ = x.shape
    assert S % ATTN_TILE == 0 and S % (ATTN_TILE // max(ATTN_DILATIONS)) == 0
    x2d = x.reshape(B * S, D_MODEL)
    for l, lw in enumerate(layers):
        qkv = _proj(x2d, lw["norm_g"], lw["w_qkv"])
        n = len(ATTN_DILATIONS)
        outs = [_attn(qkv[gi], qkv[n + gi], qkv[2 * n + gi], biases[gi], B, S, dil)
                for gi, dil in enumerate(ATTN_DILATIONS)]
        x2d = _mix(x2d, S, [o for o, _ in outs], [lse for _, lse in outs], lw, final_g,
                   final=(l == len(layers) - 1))
    return x2d.reshape(B, S, D_MODEL)


def kernel(x_prompt, x_sample, norm_g, w_in, pool_w, pool_scale, sgu_ln_g, sgu_ln_b, sgu_w, sgu_b,
           rel_bias, w_br_a, w_br_b, w_br_c, w_out, final_g):
    layers = [_layer_weights(l, norm_g, w_in, pool_w, pool_scale, sgu_ln_g, sgu_ln_b, sgu_w, sgu_b,
                             w_br_a, w_br_b, w_br_c, w_out) for l in range(DEPTH)]
    biases = [_attn_bias(rel_bias, gi, dil) for gi, dil in enumerate(ATTN_DILATIONS)]
    fg = final_g.reshape(1, D_MODEL)
    return (_encoder(x_prompt, layers, biases, fg), _encoder(x_sample, layers, biases, fg))
```

```python
import functools

import numpy as np
import jax
import jax.numpy as jnp
from jax import lax
from jax.experimental import pallas as pl
from jax.experimental.pallas import tpu as pltpu

F32 = jnp.float32
BF16 = jnp.bfloat16

D_MODEL = 1024
DEPTH = 4
POOL_WINDOWS = (2, 4, 8, 16)
POOL_HALO = 16
LANES = 128
GROUP_WIDTH = 128
N_GROUPS = 4
SGU_CHUNK = 128
HEAD_DIM = 64
HEADS_PER_GROUP = 4
ATTN_OUT = HEADS_PER_GROUP * HEAD_DIM
ATTN_DILATIONS = (1, 4, 16)
ATTN_HALF = 64
N_BUCKETS = 32
T5_MAX_DIST = 1024
EPS = 1e-6
NEG_INF = -1e30

C_XA, C_GA, C_U, C_V, C_GB, C_QKV, C_GC, C_MG, C_END = 0, 512, 1024, 1536, 2048, 2560, 4864, 5120, 8192

ROW_TILE = 512
ATTN_TILE = 2048
Q_BLOCK = 128
K_BLOCK = Q_BLOCK + 2 * ATTN_HALF
VMEM_LIMIT_BYTES = 56 * 1024 * 1024


def _rmsnorm(x, g):
    ms = jnp.mean(x * x, axis=-1, keepdims=True)
    return x * lax.rsqrt(ms + EPS) * g


def _silu(x):
    return x * (1.0 / (1.0 + jnp.exp(-x)))


def _sigmoid(x):
    return 1.0 / (1.0 + jnp.exp(-x))


def _const_spec(shape):
    nd = len(shape)
    return pl.BlockSpec(shape, lambda *_: (0,) * nd, pipeline_mode=pl.Buffered(1))


def _proj_kernel(x_ref, g_ref, w_ref, *refs):
    out_refs, scr = refs[:-1], refs[-1]
    n = len(ATTN_DILATIONS)
    h = _rmsnorm(x_ref[...], g_ref[...]).astype(BF16)
    z = jnp.dot(h, w_ref[...], preferred_element_type=F32)
    slab = 0
    for idx, o_ref in enumerate(out_refs):
        dil = ATTN_DILATIONS[idx % n]
        blk = z[:, idx * ATTN_OUT:(idx + 1) * ATTN_OUT]
        if idx < n:
            blk = blk * (HEAD_DIM ** -0.5)
        if dil == 1:
            o_ref[...] = blk.astype(BF16)
            continue
        rows = ROW_TILE // dil
        for half in range(ATTN_OUT // LANES):
            scr[slab] = blk[:, half * LANES:(half + 1) * LANES]
            for r in range(dil):
                c0 = r * ATTN_OUT + half * LANES
                o_ref[:, c0:c0 + LANES] = scr[slab, pl.ds(r, rows, stride=dil), :].astype(BF16)
            slab += 1


def _proj(x2d, g, w_qkv):
    T = x2d.shape[0]
    n = len(ATTN_DILATIONS)
    dils = [ATTN_DILATIONS[idx % n] for idx in range(3 * n)]
    n_slabs = sum(ATTN_OUT // LANES for d in dils if d > 1)
    return pl.pallas_call(
        _proj_kernel,
        grid=(T // ROW_TILE,),
        in_specs=[pl.BlockSpec((ROW_TILE, D_MODEL), lambda i: (i, 0)),
                  _const_spec((1, D_MODEL)),
                  _const_spec(w_qkv.shape)],
        out_specs=[pl.BlockSpec((ROW_TILE // d, d * ATTN_OUT), lambda i: (i, 0)) for d in dils],
        out_shape=[jax.ShapeDtypeStruct((T // d, d * ATTN_OUT), BF16) for d in dils],
        scratch_shapes=[pltpu.VMEM((n_slabs, ROW_TILE, LANES), F32)],
        compiler_params=pltpu.CompilerParams(dimension_semantics=("arbitrary",),
                                             vmem_limit_bytes=VMEM_LIMIT_BYTES),
        name="proj",
    )(x2d, g, w_qkv)


def _attn_kernel(q_ref, kc_ref, kp_ref, kn_ref, vc_ref, vp_ref, vn_ref, bias_ref, o_ref, l_ref,
                 *, dil, rows, n_tiles):
    i = pl.program_id(1)
    nb = rows // Q_BLOCK
    lane = lax.broadcasted_iota(jnp.int32, (Q_BLOCK, ATTN_OUT), 1)
    head_masks = [(lane >= HEAD_DIM * h) & (lane < HEAD_DIM * (h + 1)) for h in range(HEADS_PER_GROUP)]
    col = lax.broadcasted_iota(jnp.int32, (HEADS_PER_GROUP * Q_BLOCK, K_BLOCK), 1)
    lo_cut = jnp.where(i == 0, ATTN_HALF, 0)
    hi_cut = jnp.where(i == n_tiles - 1, K_BLOCK - ATTN_HALF, K_BLOCK)

    def window(c_ref, p_ref, n_ref, j, cs):
        lo, hi = j * Q_BLOCK - ATTN_HALF, (j + 1) * Q_BLOCK + ATTN_HALF
        parts = []
        if lo < 0:
            parts.append(p_ref[0, :, cs])
            lo = 0
        parts.append(c_ref[0, lo:min(hi, rows), cs])
        if hi > rows:
            parts.append(n_ref[0, :, cs])
        return parts[0] if len(parts) == 1 else jnp.concatenate(parts, axis=0)

    for r in range(dil):
        cs = slice(r * ATTN_OUT, (r + 1) * ATTN_OUT)
        for j in range(nb):
            rs = slice(j * Q_BLOCK, (j + 1) * Q_BLOCK)
            qb = q_ref[0, rs, cs]
            kk = window(kc_ref, kp_ref, kn_ref, j, cs)
            vv = window(vc_ref, vp_ref, vn_ref, j, cs)
            zero = jnp.zeros_like(qb)
            qs = jnp.concatenate([jnp.where(head_masks[h], qb, zero) for h in range(HEADS_PER_GROUP)], axis=0)
            s = lax.dot_general(qs, kk, (((1,), (1,)), ((), ())), preferred_element_type=F32)
            s = s + bias_ref[...]
            if j == 0:
                s = jnp.where(col < lo_cut, NEG_INF, s)
            if j == nb - 1:
                s = jnp.where(col >= hi_cut, NEG_INF, s)
            m = jnp.max(s, axis=-1, keepdims=True)
            p = jnp.exp(s - m)
            den = jnp.sum(p, axis=-1, keepdims=True)
            o = jnp.dot(p.astype(BF16), vv, preferred_element_type=F32)
            o = o * (1.0 / den)
            lse = jnp.broadcast_to(m + jnp.log(den), o.shape)
            out = jnp.zeros((Q_BLOCK, ATTN_OUT), F32)
            lout = jnp.zeros((Q_BLOCK, ATTN_OUT), F32)
            for h in range(HEADS_PER_GROUP):
                hs = slice(h * Q_BLOCK, (h + 1) * Q_BLOCK)
                out = jnp.where(head_masks[h], o[hs], out)
                lout = jnp.where(head_masks[h], lse[hs], lout)
            o_ref[0, rs, cs] = out.astype(BF16)
            l_ref[0, rs, cs] = lout


def _attn(q, k, v, bias, B, S, dil):
    L = S // dil
    width = dil * ATTN_OUT
    rows = ATTN_TILE // dil
    n_tiles = L // rows
    halo_per_tile = rows // ATTN_HALF
    n_halo = L // ATTN_HALF
    qv, kv, vv = (t.reshape(B, L, width) for t in (q, k, v))
    cur = pl.BlockSpec((1, rows, width), lambda b, i: (b, i, 0))
    prev = pl.BlockSpec((1, ATTN_HALF, width), lambda b, i: (b, jnp.maximum(i * halo_per_tile - 1, 0), 0))
    nxt = pl.BlockSpec((1, ATTN_HALF, width),
                       lambda b, i: (b, jnp.minimum((i + 1) * halo_per_tile, n_halo - 1), 0))
    o, l = pl.pallas_call(
        functools.partial(_attn_kernel, dil=dil, rows=rows, n_tiles=n_tiles),
        grid=(B, n_tiles),
        in_specs=[cur, cur, prev, nxt, cur, prev, nxt, _const_spec(bias.shape)],
        out_specs=[cur, cur],
        out_shape=[jax.ShapeDtypeStruct((B, L, width), BF16), jax.ShapeDtypeStruct((B, L, width), F32)],
        compiler_params=pltpu.CompilerParams(dimension_semantics=("arbitrary", "arbitrary"),
                                             vmem_limit_bytes=VMEM_LIMIT_BYTES),
        name=f"attn_d{dil}",
    )(qv, kv, kv, kv, vv, vv, vv, bias)
    return o.reshape(B * L, width), l.reshape(B * L, width)


def _mix_kernel(x_ref, xp_ref, xn_ref, o0_ref, o1_ref, o2_ref, l0_ref, l1_ref, l2_ref,
                g_ref, w_in_ref, pool_w_ref, pool_scale_ref, ln_g_ref, ln_b_ref, sgu_w_ref, sgu_b_ref,
                w_a_ref, w_b_ref, w_c_ref, w_out_ref, fg_ref, y_ref, xa_scr, il_scr,
                *, seq_len, final):
    TM = ROW_TILE
    tiles_per_seq = seq_len // TM
    pos0 = (pl.program_id(0) % tiles_per_seq) * TM
    g = g_ref[...]
    x = x_ref[...]
    h = _rmsnorm(x, g).astype(BF16)

    def zcols(a, b):
        return jnp.dot(h, w_in_ref[:, a:b], preferred_element_type=F32)


    hp = _rmsnorm(xp_ref[...], g).astype(BF16)
    hn = _rmsnorm(xn_ref[...], g).astype(BF16)
    h_ext = jnp.concatenate([hp, h, hn], axis=0)
    xa_ext = jnp.dot(h_ext, w_in_ref[:, C_XA:C_GA], preferred_element_type=F32)
    v = zcols(C_V, C_GB)
    u = zcols(C_U, C_V)
    row = lax.broadcasted_iota(jnp.int32, xa_ext.shape, 0) + (pos0 - POOL_HALO)
    xa_ext = jnp.where((row >= 0) & (row < seq_len), xa_ext, 0.0)
    xa_scr[...] = xa_ext
    gate_a = _sigmoid(zcols(C_MG, C_MG + D_MODEL))

    t = lax.broadcasted_iota(jnp.int32, (TM, GROUP_WIDTH), 0) + pos0
    mixed = []
    for gi, w in enumerate(POOL_WINDOWS):
        gs = slice(gi * GROUP_WIDTH, (gi + 1) * GROUP_WIDTH)
        acc = xa_scr[pl.ds(POOL_HALO - w // 2, TM), gs]
        for off in range(-w // 2 + 1, w // 2):
            acc = acc + xa_scr[pl.ds(POOL_HALO + off, TM), gs]
        cnt = (jnp.minimum(t + (w // 2 - 1), seq_len - 1) - jnp.maximum(t - w // 2, 0) + 1).astype(F32)
        mixed.append((acc / cnt - xa_scr[pl.ds(POOL_HALO, TM), gs]).astype(BF16))

    mu = jnp.mean(v, axis=-1, keepdims=True)
    vc = v - mu
    var = jnp.mean(vc * vc, axis=-1, keepdims=True)
    vn = (vc * lax.rsqrt(var + EPS) * ln_g_ref[...] + ln_b_ref[...]).astype(BF16)
    silu_ga = _silu(zcols(C_GA, C_U))
    silu_gb = _silu(zcols(C_GB, C_QKV))

    a_mix = jnp.concatenate([jnp.dot(jnp.concatenate(mixed[2 * i:2 * i + 2], axis=1), pool_w_ref[i],
                                     preferred_element_type=F32)
                             for i in range(N_GROUPS // 2)], axis=1) * pool_scale_ref[...]
    a_out = (a_mix * silu_ga).astype(BF16)
    gate_b = _sigmoid(zcols(C_MG + D_MODEL, C_MG + 2 * D_MODEL))

    n_chunks = TM // SGU_CHUNK
    sp_cols = []
    for gi in range(N_GROUPS):
        gs = slice(gi * GROUP_WIDTH, (gi + 1) * GROUP_WIDTH)
        rhs = jnp.concatenate([vn[c * SGU_CHUNK:(c + 1) * SGU_CHUNK, gs] for c in range(n_chunks)], axis=1)
        sp_g = jnp.dot(sgu_w_ref[gi], rhs, preferred_element_type=F32)
        sp_cols.append(jnp.concatenate(
            [sp_g[:, c * GROUP_WIDTH:(c + 1) * GROUP_WIDTH] + sgu_b_ref[gi] for c in range(n_chunks)], axis=0))
    sp = jnp.concatenate(sp_cols, axis=1)
    merged = gate_a * jnp.dot(a_out, w_a_ref[...], preferred_element_type=F32)
    b_out = (u * sp * silu_gb).astype(BF16)
    silu_gc = _silu(zcols(C_GC, C_MG))
    gate_c = _sigmoid(zcols(C_MG + 2 * D_MODEL, C_END))
    merged = merged + gate_b * jnp.dot(b_out, w_b_ref[...], preferred_element_type=F32)

    def token_order(ref, gi, half, slab):
        dil = ATTN_DILATIONS[gi]
        if dil == 1:
            return ref[:, half * LANES:(half + 1) * LANES].astype(F32)
        for r in range(dil):
            c0 = r * ATTN_OUT + half * LANES
            il_scr[slab, pl.ds(r, TM // dil, stride=dil), :] = ref[:, c0:c0 + LANES].astype(F32)
        return il_scr[slab]

    c_halves = []
    slab = 0
    for half in range(ATTN_OUT // LANES):
        os_, ls_ = [], []
        for gi, (o_ref, l_ref) in enumerate(((o0_ref, l0_ref), (o1_ref, l1_ref), (o2_ref, l2_ref))):
            os_.append(token_order(o_ref, gi, half, slab))
            ls_.append(token_order(l_ref, gi, half, slab + 1))
            if ATTN_DILATIONS[gi] > 1:
                slab += 2
        lmax = jnp.maximum(jnp.maximum(ls_[0], ls_[1]), ls_[2])
        es = [jnp.exp(l - lmax) for l in ls_]
        c_halves.append((es[0] * os_[0] + es[1] * os_[1] + es[2] * os_[2]) / (es[0] + es[1] + es[2]))
    c = jnp.concatenate(c_halves, axis=1)
    c_out = (c * silu_gc).astype(BF16)
    merged = merged + gate_c * jnp.dot(c_out, w_c_ref[...], preferred_element_type=F32)

    y = x + jnp.dot(merged.astype(BF16), w_out_ref[...], preferred_element_type=F32)
    if final:
        y = _rmsnorm(y, fg_ref[...])
    y_ref[...] = y


def _mix(x2d, seq_len, attn_o, attn_l, lw, final_g, final):
    T = x2d.shape[0]
    halo_per_tile = ROW_TILE // POOL_HALO
    n_halo = T // POOL_HALO
    row = lambda width: pl.BlockSpec((ROW_TILE, width), lambda i: (i, 0))
    attn_spec = lambda d: pl.BlockSpec((ROW_TILE // d, d * ATTN_OUT), lambda i: (i, 0))
    n_il_slabs = 2 * (ATTN_OUT // LANES) * sum(d > 1 for d in ATTN_DILATIONS)
    prev =pl.BlockSpec((POOL_HALO, D_MODEL), lambda i: (jnp.maximum(i * halo_per_tile - 1, 0), 0))
    nxt = pl.BlockSpec((POOL_HALO, D_MODEL), lambda i: (jnp.minimum((i + 1) * halo_per_tile, n_halo - 1), 0))
    weights = [lw["norm_g"], lw["w_in"], lw["pool_w"], lw["pool_scale"], lw["ln_g"], lw["ln_b"],
               lw["sgu_w"], lw["sgu_b"], lw["w_a"], lw["w_b"], lw["w_c"], lw["w_out"], final_g]
    return pl.pallas_call(
        functools.partial(_mix_kernel, seq_len=seq_len, final=final),
        grid=(T // ROW_TILE,),
        in_specs=[row(D_MODEL), prev, nxt] + 2 * [attn_spec(d) for d in ATTN_DILATIONS]
        + [_const_spec(w.shape) for w in weights],
        out_specs=row(D_MODEL),
        out_shape=jax.ShapeDtypeStruct((T, D_MODEL), F32),
        scratch_shapes=[pltpu.VMEM((ROW_TILE + 2 * POOL_HALO, 4 * GROUP_WIDTH), F32),
                        pltpu.VMEM((n_il_slabs, ROW_TILE, LANES), F32)],
        compiler_params=pltpu.CompilerParams(dimension_semantics=("arbitrary",),
                                             vmem_limit_bytes=VMEM_LIMIT_BYTES),
        name="mix_final" if final else "mix",
    )(x2d, x2d, x2d, *attn_o, *attn_l, *weights)


def _t5_bucket(rel):
    half = N_BUCKETS // 2
    n = -rel
    ret = (n < 0).astype(np.int32) * half
    n = np.abs(n)
    max_exact = half // 2
    large = max_exact + (np.log(np.maximum(n, 1) / max_exact) / np.log(T5_MAX_DIST / max_exact)
                         * (half - max_exact)).astype(np.int32)
    large = np.minimum(large, half - 1)
    return (ret + np.where(n < max_exact, n, large)).astype(np.int32)


def _attn_bias(rel_bias, gi, dil):
    rel = np.arange(K_BLOCK)[None, :] - ATTN_HALF - np.arange(Q_BLOCK)[:, None]
    table = rel_bias[:, gi * HEADS_PER_GROUP:(gi + 1) * HEADS_PER_GROUP].astype(F32)
    onehot = (_t5_bucket(rel * dil)[..., None] == np.arange(N_BUCKETS)).astype(np.float32)
    bias = jnp.einsum("qkb,bh->hqk", onehot, table, precision=lax.Precision.HIGHEST)
    bias = jnp.where((np.abs(rel) <= ATTN_HALF)[None], bias, NEG_INF)
    return bias.reshape(HEADS_PER_GROUP * Q_BLOCK, K_BLOCK)


def _pair_block_diag(w):
    G, C, _ = w.shape
    z = jnp.zeros((G // 2, C, C), w.dtype)
    top = jnp.concatenate([w[0::2], z], axis=2)
    bot = jnp.concatenate([z, w[1::2]], axis=2)
    return jnp.concatenate([top, bot], axis=1)


def _layer_weights(l, norm_g, w_in, pool_w, pool_scale, sgu_ln_g, sgu_ln_b, sgu_w, sgu_b,
                   w_br_a, w_br_b, w_br_c, w_out):
    return {
        "norm_g": norm_g[l].reshape(1, D_MODEL),
        "w_in": w_in[l].astype(BF16),
        "w_qkv": w_in[l, :, C_QKV:C_GC].astype(BF16),
        "pool_w": _pair_block_diag(pool_w[l].astype(BF16)),
        "pool_scale": pool_scale[l].reshape(1, -1),
        "ln_g": sgu_ln_g[l].reshape(1, -1),
        "ln_b": sgu_ln_b[l].reshape(1, -1),
        "sgu_w": sgu_w[l].astype(BF16),
        "sgu_b": jnp.broadcast_to(sgu_b[l][:, :, None], (N_GROUPS, SGU_CHUNK, GROUP_WIDTH)),
        "w_a": w_br_a[l].astype(BF16),
        "w_b": w_br_b[l].astype(BF16),
        "w_c": w_br_c[l].astype(BF16),
        "w_out": w_out[l].astype(BF16),
    }


def _encoder(x, layers, biases, final_g):
    B, S, _ = x.shape
    assert S % ATTN_TILE == 0 and S % (ATTN_TILE // max(ATTN_DILATIONS)) == 0
    x2d = x.reshape(B * S, D_MODEL)
    for l, lw in enumerate(layers):
        qkv = _proj(x2d, lw["norm_g"], lw["w_qkv"])
        n = len(ATTN_DILATIONS)
        outs = [_attn(qkv[gi], qkv[n + gi], qkv[2 * n + gi], biases[gi], B, S, dil)
                for gi, dil in enumerate(ATTN_DILATIONS)]
        x2d = _mix(x2d, S, [o for o, _ in outs], [lse for _, lse in outs], lw, final_g,
                   final=(l == len(layers) - 1))
    return x2d.reshape(B, S, D_MODEL)


def kernel(x_prompt, x_sample, norm_g, w_in, pool_w, pool_scale, sgu_ln_g, sgu_ln_b, sgu_w, sgu_b,
           rel_bias, w_br_a, w_br_b, w_br_c, w_out, final_g):
    layers = [_layer_weights(l, norm_g, w_in, pool_w, pool_scale, sgu_ln_g, sgu_ln_b, sgu_w, sgu_b,
                             w_br_a, w_br_b, w_br_c, w_out) for l in range(DEPTH)]
    biases = [_attn_bias(rel_bias, gi, dil) for gi, dil in enumerate(ATTN_DILATIONS)]
    fg = final_g.reshape(1, D_MODEL)
    return (_encoder(x_prompt, layers, biases, fg), _encoder(x_sample, layers, biases, fg))
```

```python
import functools

import numpy as np
import jax
import jax.numpy as jnp
from jax import lax
from jax.experimental import pallas as pl
from jax.experimental.pallas import tpu as pltpu

F32 = jnp.float32
BF16 = jnp.bfloat16

D_MODEL = 1024
DEPTH = 4
POOL_WINDOWS = (2, 4, 8, 16)
POOL_HALO = 16
LANES = 128
GROUP_WIDTH = 128
N_GROUPS = 4
SGU_CHUNK = 128
HEAD_DIM = 64
HEADS_PER_GROUP = 4
ATTN_OUT = HEADS_PER_GROUP * HEAD_DIM
ATTN_DILATIONS = (1, 4, 16)
ATTN_HALF = 64
N_BUCKETS = 32
T5_MAX_DIST = 1024
EPS = 1e-6
NEG_INF = -1e30

R_QKV, R_GC = 2560, 4864
C_XA, C_GA, C_U, C_V, C_GB, C_GC, C_MG, C_END = 0, 512, 1024, 1536, 2048, 2560, 2816, 5888

QKV_ORDER = tuple((gi, t) for gi in reversed(range(len(ATTN_DILATIONS))) for t in range(3))
N_QKV_SLABS = (ATTN_OUT // LANES) * sum(ATTN_DILATIONS[gi] > 1 for gi, _ in QKV_ORDER)

ROW_TILE = 512
ATTN_TILE = 2048
Q_BLOCK = 128
K_BLOCK = Q_BLOCK + 2 * ATTN_HALF
VMEM_LIMIT_BYTES = 56 * 1024 * 1024


def _rmsnorm(x, g):
    ms = jnp.mean(x * x, axis=-1, keepdims=True)
    return x * lax.rsqrt(ms + EPS) * g


def _silu(x):
    return x * (1.0 / (1.0 + jnp.exp(-x)))


def _sigmoid(x):
    return 1.0 / (1.0 + jnp.exp(-x))


def _const_spec(shape):
    nd = len(shape)
    return pl.BlockSpec(shape, lambda *_: (0,) * nd, pipeline_mode=pl.Buffered(1))


def _row_spec(rows, width):
    return pl.BlockSpec((rows, width), lambda i: (i, 0))


def _emit_qkv(h, w_ref, out_refs, scr):
    z = jnp.dot(h, w_ref[...], preferred_element_type=F32)
    slab = 0
    for idx, ((gi, t), o_ref) in enumerate(zip(QKV_ORDER, out_refs)):
        dil = ATTN_DILATIONS[gi]
        blk = z[:, idx * ATTN_OUT:(idx + 1) * ATTN_OUT]
        if t == 0:
            blk = blk * (HEAD_DIM ** -0.5)
        if dil == 1:
            o_ref[...] = blk.astype(BF16)
            continue
        rows = ROW_TILE // dil
        for half in range(ATTN_OUT // LANES):
            scr[slab] = blk[:, half * LANES:(half + 1) * LANES]
            for r in range(dil):
                c0 = r * ATTN_OUT + half * LANES
                o_ref[:, c0:c0 + LANES] = scr[slab, pl.ds(r, rows, stride=dil), :].astype(BF16)
            slab += 1


def _qkv_out(T):
    dils = [ATTN_DILATIONS[gi] for gi, _ in QKV_ORDER]
    specs = [_row_spec(ROW_TILE // d, d * ATTN_OUT) for d in dils]
    shapes = [jax.ShapeDtypeStruct((T // d, d * ATTN_OUT), BF16) for d in dils]
    return specs, shapes


def _proj_kernel(x_ref, g_ref, w_ref, h_ref, *refs):
    h = _rmsnorm(x_ref[...], g_ref[...]).astype(BF16)
    h_ref[...] = h
    _emit_qkv(h, w_ref, refs[:-1], refs[-1])


def _proj(x2d, g, w_qkv):
    T = x2d.shape[0]
    qkv_specs, qkv_shapes = _qkv_out(T)
    return pl.pallas_call(
        _proj_kernel,
        grid=(T // ROW_TILE,),
        in_specs=[_row_spec(ROW_TILE, D_MODEL), _const_spec((1, D_MODEL)), _const_spec(w_qkv.shape)],
        out_specs=[_row_spec(ROW_TILE, D_MODEL)] + qkv_specs,
        out_shape=[jax.ShapeDtypeStruct((T, D_MODEL), BF16)] + qkv_shapes,
        scratch_shapes=[pltpu.VMEM((N_QKV_SLABS, ROW_TILE, LANES), F32)],
        compiler_params=pltpu.CompilerParams(dimension_semantics=("arbitrary",),
                                             vmem_limit_bytes=VMEM_LIMIT_BYTES),
        name="proj",
    )(x2d, g, w_qkv)


def _attn_kernel(q_ref, kc_ref, kp_ref, kn_ref, vc_ref, vp_ref, vn_ref, bias_ref, o_ref, l_ref,
                 *, dil, rows, n_tiles):
    i = pl.program_id(1)
    nb = rows // Q_BLOCK
    lane = lax.broadcasted_iota(jnp.int32, (Q_BLOCK, ATTN_OUT), 1)
    head_masks = [(lane >= HEAD_DIM * h) & (lane < HEAD_DIM * (h + 1)) for h in range(HEADS_PER_GROUP)]
    col = lax.broadcasted_iota(jnp.int32, (HEADS_PER_GROUP * Q_BLOCK, K_BLOCK), 1)
    lo_cut = jnp.where(i == 0, ATTN_HALF, 0)
    hi_cut = jnp.where(i == n_tiles - 1, K_BLOCK - ATTN_HALF, K_BLOCK)

    def window(c_ref, p_ref, n_ref, j, cs):
        lo, hi = j * Q_BLOCK - ATTN_HALF, (j + 1) * Q_BLOCK + ATTN_HALF
        parts = []
        if lo < 0:
            parts.append(p_ref[0, :, cs])
            lo = 0
        parts.append(c_ref[0, lo:min(hi, rows), cs])
        if hi > rows:
            parts.append(n_ref[0, :, cs])
        return parts[0] if len(parts) == 1 else jnp.concatenate(parts, axis=0)

    for r in range(dil):
        cs = slice(r * ATTN_OUT, (r + 1) * ATTN_OUT)
        for j in range(nb):
            rs = slice(j * Q_BLOCK, (j + 1) * Q_BLOCK)
            qb = q_ref[0, rs, cs]
            kk = window(kc_ref, kp_ref, kn_ref, j, cs)
            vv = window(vc_ref, vp_ref, vn_ref, j, cs)
            zero = jnp.zeros_like(qb)
            qs = jnp.concatenate([jnp.where(head_masks[h], qb, zero) for h in range(HEADS_PER_GROUP)], axis=0)
            s = lax.dot_general(qs, kk, (((1,), (1,)), ((), ())), preferred_element_type=F32)
            s = s + bias_ref[...]
            if j == 0:
                s = jnp.where(col < lo_cut, NEG_INF, s)
            if j == nb - 1:
                s = jnp.where(col >= hi_cut, NEG_INF, s)
            m = jnp.max(s, axis=-1, keepdims=True)
            p = jnp.exp(s - m)
            den = jnp.sum(p, axis=-1, keepdims=True)
            o = jnp.dot(p.astype(BF16), vv, preferred_element_type=F32)
            o = o * (1.0 / den)
            lse = jnp.broadcast_to(m + jnp.log(den), o.shape)
            out = jnp.zeros((Q_BLOCK, ATTN_OUT), F32)
            lout = jnp.zeros((Q_BLOCK, ATTN_OUT), F32)
            for h in range(HEADS_PER_GROUP):
                hs = slice(h * Q_BLOCK, (h + 1) * Q_BLOCK)
                out = jnp.where(head_masks[h], o[hs], out)
                lout = jnp.where(head_masks[h], lse[hs], lout)
            o_ref[0, rs, cs] = out.astype(BF16)
            l_ref[0, rs, cs] = lout


def _attn(q, k, v, bias, B, S, dil):
    L = S // dil
    width = dil * ATTN_OUT
    rows = ATTN_TILE // dil
    n_tiles = L // rows
    halo_per_tile = rows // ATTN_HALF
    n_halo = L // ATTN_HALF
    qv, kv, vv = (t.reshape(B, L, width) for t in (q, k, v))
    cur = pl.BlockSpec((1, rows, width), lambda b, i: (b, i, 0))
    prev = pl.BlockSpec((1, ATTN_HALF, width), lambda b, i: (b, jnp.maximum(i * halo_per_tile - 1, 0), 0))
    nxt = pl.BlockSpec((1, ATTN_HALF, width),
                       lambda b, i: (b, jnp.minimum((i + 1) * halo_per_tile, n_halo - 1), 0))
    o, l = pl.pallas_call(
        functools.partial(_attn_kernel, dil=dil, rows=rows, n_tiles=n_tiles),
        grid=(B, n_tiles),
        in_specs=[cur, cur, prev, nxt, cur, prev, nxt, _const_spec(bias.shape)],
        out_specs=[cur, cur],
        out_shape=[jax.ShapeDtypeStruct((B, L, width), BF16), jax.ShapeDtypeStruct((B, L, width), F32)],
        compiler_params=pltpu.CompilerParams(dimension_semantics=("arbitrary", "arbitrary"),
                                             vmem_limit_bytes=VMEM_LIMIT_BYTES),
        name=f"attn_d{dil}",
    )(qv, kv, kv, kv, vv, vv, vv, bias)
    return o.reshape(B * L, width), l.reshape(B * L, width)


def _mix_kernel(x_ref, h_ref, hp_ref, hn_ref, o0_ref, o1_ref, o2_ref, l0_ref, l1_ref, l2_ref,
                w_in_ref, pool_w_ref, pool_scale_ref, ln_g_ref, ln_b_ref, sgu_w_ref, sgu_b_ref,
                w_a_ref, w_b_ref, w_c_ref, w_out_ref, g_next_ref, *refs, seq_len, final):
    if final:
        (y_ref, xa_scr, il_scr), w_qkv_ref = refs, None
    else:
        w_qkv_ref, y_ref, h_next_ref, *qkv_refs, xa_scr, il_scr = refs
    TM = ROW_TILE
    tiles_per_seq = seq_len // TM
    pos0 = (pl.program_id(0) % tiles_per_seq) * TM
    h = h_ref[...]

    def zcols(a, b):
        return jnp.dot(h, w_in_ref[:, a:b], preferred_element_type=F32)


    h_ext = jnp.concatenate([hp_ref[...], h, hn_ref[...]], axis=0)
    xa_ext = jnp.dot(h_ext, w_in_ref[:, C_XA:C_GA], preferred_element_type=F32)
    v = zcols(C_V, C_GB)
    u = zcols(C_U, C_V)
    row = lax.broadcasted_iota(jnp.int32, xa_ext.shape, 0) + (pos0 - POOL_HALO)
    xa_ext = jnp.where((row >= 0) & (row < seq_len), xa_ext, 0.0)
    xa_scr[...] = xa_ext

    def token_order(ref, gi, half, slab):
        dil = ATTN_DILATIONS[gi]
        if dil == 1:
            return ref[:, half * LANES:(half + 1) * LANES].astype(F32)
        for r in range(dil):
            c0 = r * ATTN_OUT + half * LANES
            il_scr[slab, pl.ds(r, TM // dil, stride=dil), :] = ref[:, c0:c0 + LANES].astype(F32)
        return il_scr[slab]

    c_halves = []
    slab = 0
    for half in range(ATTN_OUT // LANES):
        os_, ls_ = [], []
        for gi, (o_ref, l_ref) in enumerate(((o0_ref, l0_ref), (o1_ref, l1_ref), (o2_ref, l2_ref))):
            os_.append(token_order(o_ref, gi, half, slab))
            ls_.append(token_order(l_ref, gi, half, slab + 1))
            if ATTN_DILATIONS[gi] > 1:
                slab += 2
        lmax = jnp.maximum(jnp.maximum(ls_[0], ls_[1]), ls_[2])
        es = [jnp.exp(l - lmax) for l in ls_]
        c_halves.append((es[0] * os_[0] + es[1] * os_[1] + es[2] * os_[2]) / (es[0] + es[1] + es[2]))
    c = jnp.concatenate(c_halves, axis=1)

    gate_a = _sigmoid(zcols(C_MG, C_MG + D_MODEL))

    t = lax.broadcasted_iota(jnp.int32, (TM, GROUP_WIDTH), 0) + pos0
    mixed = []
    for gi, w in enumerate(POOL_WINDOWS):
        gs = slice(gi * GROUP_WIDTH, (gi + 1) * GROUP_WIDTH)
        acc = xa_scr[pl.ds(POOL_HALO - w // 2, TM), gs]
        for off in range(-w // 2 + 1, w // 2):
            acc = acc + xa_scr[pl.ds(POOL_HALO + off, TM), gs]
        cnt = (jnp.minimum(t + (w // 2 - 1), seq_len - 1) - jnp.maximum(t - w // 2, 0) + 1).astype(F32)
        mixed.append((acc / cnt - xa_scr[pl.ds(POOL_HALO, TM), gs]).astype(BF16))

    mu = jnp.mean(v, axis=-1, keepdims=True)
    vc = v - mu
    var = jnp.mean(vc * vc, axis=-1, keepdims=True)
    vn = (vc * lax.rsqrt(var + EPS) * ln_g_ref[...] + ln_b_ref[...]).astype(BF16)
    silu_ga = _silu(zcols(C_GA, C_U))
    silu_gb = _silu(zcols(C_GB, C_GC))

    a_mix = jnp.concatenate([jnp.dot(jnp.concatenate(mixed[2 * i:2 * i + 2], axis=1), pool_w_ref[i],
                                     preferred_element_type=F32)
                             for i in range(N_GROUPS // 2)], axis=1) * pool_scale_ref[...]
    a_out = (a_mix * silu_ga).astype(BF16)
    gate_b = _sigmoid(zcols(C_MG + D_MODEL, C_MG + 2 * D_MODEL))

    n_chunks = TM // SGU_CHUNK
    sp_cols = []
    for gi in range(N_GROUPS):
        gs = slice(gi * GROUP_WIDTH, (gi + 1) * GROUP_WIDTH)
        rhs = jnp.concatenate([vn[c_ * SGU_CHUNK:(c_ + 1) * SGU_CHUNK, gs] for c_ in range(n_chunks)], axis=1)
        sp_g = jnp.dot(sgu_w_ref[gi], rhs, preferred_element_type=F32)
        sp_cols.append(jnp.concatenate(
            [sp_g[:, c_ * GROUP_WIDTH:(c_ + 1) * GROUP_WIDTH] + sgu_b_ref[gi] for c_ in range(n_chunks)], axis=0))
    sp = jnp.concatenate(sp_cols, axis=1)
    merged = gate_a * jnp.dot(a_out, w_a_ref[...], preferred_element_type=F32)
    b_out = (u * sp * silu_gb).astype(BF16)
    silu_gc = _silu(zcols(C_GC, C_MG))
    gate_c = _sigmoid(zcols(C_MG + 2 * D_MODEL, C_END))
    merged = merged + gate_b * jnp.dot(b_out, w_b_ref[...], preferred_element_type=F32)

    c_out = (c * silu_gc).astype(BF16)
    merged = merged + gate_c * jnp.dot(c_out, w_c_ref[...], preferred_element_type=F32)

    y = x_ref[...] + jnp.dot(merged.astype(BF16), w_out_ref[...], preferred_element_type=F32)
    if final:
        y_ref[...] = _rmsnorm(y, g_next_ref[...])
        return
    y_ref[...] = y
    h_next = _rmsnorm(y, g_next_ref[...]).astype(BF16)
    h_next_ref[...] = h_next
    _emit_qkv(h_next, w_qkv_ref, qkv_refs, il_scr)


def _mix(x2d, h2d, seq_len, attn_o, attn_l, lw, g_next, w_qkv_next):
    final = w_qkv_next is None
    T = x2d.shape[0]
    halo_per_tile = ROW_TILE // POOL_HALO
    n_halo = T // POOL_HALO
    n_il_slabs = 2 * (ATTN_OUT // LANES) * sum(d > 1 for d in ATTN_DILATIONS)
    prev = pl.BlockSpec((POOL_HALO, D_MODEL), lambda i: (jnp.maximum(i * halo_per_tile - 1, 0), 0))
    nxt = pl.BlockSpec((POOL_HALO, D_MODEL), lambda i: (jnp.minimum((i + 1) * halo_per_tile, n_halo - 1), 0))
    weights = [lw["w_in"], lw["pool_w"], lw["pool_scale"], lw["ln_g"], lw["ln_b"],
               lw["sgu_w"], lw["sgu_b"], lw["w_a"], lw["w_b"], lw["w_c"], lw["w_out"], g_next]
    out_specs = [_row_spec(ROW_TILE, D_MODEL)]
    out_shape = [jax.ShapeDtypeStruct((T, D_MODEL), F32)]
    if not final:
        weights.append(w_qkv_next)
        qkv_specs, qkv_shapes = _qkv_out(T)
        out_specs += [_row_spec(ROW_TILE, D_MODEL)] + qkv_specs
        out_shape += [jax.ShapeDtypeStruct((T, D_MODEL), BF16)] + qkv_shapes
        n_il_slabs = max(n_il_slabs, N_QKV_SLABS)
    scratch = [pltpu.VMEM((ROW_TILE + 2 * POOL_HALO, N_GROUPS * GROUP_WIDTH), F32),
               pltpu.VMEM((n_il_slabs, ROW_TILE, LANES), F32)]
    outs = pl.pallas_call(
        functools.partial(_mix_kernel, seq_len=seq_len, final=final),
        grid=(T // ROW_TILE,),
        in_specs=[_row_spec(ROW_TILE, D_MODEL), _row_spec(ROW_TILE, D_MODEL), prev, nxt]
        + 2 * [_row_spec(ROW_TILE // d, d * ATTN_OUT) for d in ATTN_DILATIONS]
        + [_const_spec(w.shape) for w in weights],
        out_specs=out_specs,
        out_shape=out_shape,
        scratch_shapes=scratch,
        compiler_params=pltpu.CompilerParams(dimension_semantics=("arbitrary",),
                                             vmem_limit_bytes=VMEM_LIMIT_BYTES),
        name="mix_final" if final else "mix",
    )(x2d, h2d, h2d, h2d, *attn_o, *attn_l, *weights)
    return outs


def _t5_bucket(rel):
    half = N_BUCKETS // 2
    n = -rel
    ret = (n < 0).astype(np.int32) * half
    n = np.abs(n)
    max_exact = half // 2
    large = max_exact + (np.log(np.maximum(n, 1) / max_exact) / np.log(T5_MAX_DIST / max_exact)
                         * (half - max_exact)).astype(np.int32)
    large = np.minimum(large, half - 1)
    return (ret + np.where(n < max_exact, n, large)).astype(np.int32)


def _attn_bias(rel_bias, gi, dil):
    rel = np.arange(K_BLOCK)[None, :] - ATTN_HALF - np.arange(Q_BLOCK)[:, None]
    table = rel_bias[:, gi * HEADS_PER_GROUP:(gi + 1) * HEADS_PER_GROUP].astype(F32)
    onehot = (_t5_bucket(rel * dil)[..., None] == np.arange(N_BUCKETS)).astype(np.float32)
    bias = jnp.einsum("qkb,bh->hqk", onehot, table, precision=lax.Precision.HIGHEST)
    bias = jnp.where((np.abs(rel) <= ATTN_HALF)[None], bias, NEG_INF)
    return bias.reshape(HEADS_PER_GROUP * Q_BLOCK, K_BLOCK)


def _pair_block_diag(w):
    G, C, _ = w.shape
    z = jnp.zeros((G // 2, C, C), w.dtype)
    top = jnp.concatenate([w[0::2], z], axis=2)
    bot = jnp.concatenate([z, w[1::2]], axis=2)
    return jnp.concatenate([top, bot], axis=1)


def _qkv_weights(w_in_l):
    cols = []
    for gi, t in QKV_ORDER:
        c0 = R_QKV + t * len(ATTN_DILATIONS) * ATTN_OUT + gi * ATTN_OUT
        cols.append(w_in_l[:, c0:c0 + ATTN_OUT])
    return jnp.concatenate(cols, axis=1).astype(BF16)


def _layer_weights(l, norm_g, w_in, pool_w, pool_scale, sgu_ln_g, sgu_ln_b, sgu_w, sgu_b,
                   w_br_a, w_br_b, w_br_c, w_out):
    return {
        "norm_g": norm_g[l].reshape(1, D_MODEL),
        "w_in": jnp.concatenate([w_in[l, :, :R_QKV], w_in[l, :, R_GC:]], axis=1).astype(BF16),
        "w_qkv": _qkv_weights(w_in[l]),
        "pool_w": _pair_block_diag(pool_w[l].astype(BF16)),
        "pool_scale": pool_scale[l].reshape(1, -1),
        "ln_g": sgu_ln_g[l].reshape(1, -1),
        "ln_b": sgu_ln_b[l].reshape(1, -1),
        "sgu_w": sgu_w[l].astype(BF16),
        "sgu_b": jnp.broadcast_to(sgu_b[l][:, :, None], (N_GROUPS, SGU_CHUNK, GROUP_WIDTH)),
        "w_a": w_br_a[l].astype(BF16),
        "w_b": w_br_b[l].astype(BF16),
        "w_c": w_br_c[l].astype(BF16),
        "w_out": w_out[l].astype(BF16),
    }


def _encoder(x, layers, biases, final_g):
    B, S, _ = x.shape
    assert S % ATTN_TILE == 0 and S % (ATTN_TILE // max(ATTN_DILATIONS)) == 0
    x2d = x.reshape(B * S, D_MODEL)
    h2d, *qkv = _proj(x2d, layers[0]["norm_g"], layers[0]["w_qkv"])
    for l, lw in enumerate(layers):
        by_group = {key: arr for key, arr in zip(QKV_ORDER, qkv)}
        outs = [_attn(by_group[gi, 0], by_group[gi, 1], by_group[gi, 2], biases[gi], B, S, dil)
                for gi, dil in enumerate(ATTN_DILATIONS)]
        attn_o, attn_l = [o for o, _ in outs], [lse for _, lse in outs]
        if l + 1 < len(layers):
            x2d, h2d, *qkv = _mix(x2d, h2d, S, attn_o, attn_l, lw,
                                  layers[l + 1]["norm_g"], layers[l + 1]["w_qkv"])
        else:
            (x2d,) = _mix(x2d, h2d, S, attn_o, attn_l, lw, final_g, None)
    return x2d.reshape(B, S, D_MODEL)


def kernel(x_prompt, x_sample, norm_g, w_in, pool_w, pool_scale, sgu_ln_g, sgu_ln_b, sgu_w, sgu_b,
           rel_bias, w_br_a, w_br_b, w_br_c, w_out, final_g):
    layers = [_layer_weights(l, norm_g, w_in, pool_w, pool_scale, sgu_ln_g, sgu_ln_b, sgu_w, sgu_b,
                             w_br_a, w_br_b, w_br_c, w_out) for l in range(DEPTH)]
    biases = [_attn_bias(rel_bias, gi, dil) for gi, dil in enumerate(ATTN_DILATIONS)]
    fg = final_g.reshape(1, D_MODEL)
    return (_encoder(x_prompt, layers, biases, fg), _encoder(x_sample, layers, biases, fg))
```

```python
import functools

import numpy as np
import jax
import jax.numpy as jnp
from jax import lax
from jax.experimental import pallas as pl
from jax.experimental.pallas import tpu as pltpu

F32 = jnp.float32
BF16 = jnp.bfloat16

D_MODEL = 1024
DEPTH = 4
POOL_WINDOWS = (2, 4, 8, 16)
POOL_HALO = 16
LANES = 128
GROUP_WIDTH = 128
N_GROUPS = 4
SGU_CHUNK = 128
HEAD_DIM = 64
HEADS_PER_GROUP = 4
ATTN_OUT = HEADS_PER_GROUP * HEAD_DIM
ATTN_DILATIONS = (1, 4, 16)
ATTN_HALF = 64
N_BUCKETS = 32
T5_MAX_DIST = 1024
EPS = 1e-6
NEG_INF = -1e30

R_QKV, R_GC = 2560, 4864
C_XA, C_GA, C_U, C_V, C_GB, C_GC, C_MG, C_END = 0, 512, 1024, 1536, 2048, 2560, 2816, 5888

QKV_ORDER = tuple((gi, t) for t in range(3) for gi in (2, 1, 0))
STRIDE_STEP = 4
N_QKV_SLABS = (ATTN_OUT // LANES) * sum(ATTN_DILATIONS[gi] > 1 for gi, _ in QKV_ORDER) + 2

ROW_TILE = 512
ATTN_TILE = 2048
Q_BLOCK = 128
K_BLOCK = Q_BLOCK + 2 * ATTN_HALF
VMEM_LIMIT_BYTES = 56 * 1024 * 1024


def _rmsnorm(x, g):
    ms = jnp.mean(x * x, axis=-1, keepdims=True)
    return x * lax.rsqrt(ms + EPS) * g


def _silu(x):
    return x * (1.0 / (1.0 + jnp.exp(-x)))


def _sigmoid(x):
    return 1.0 / (1.0 + jnp.exp(-x))


def _const_spec(shape):
    nd = len(shape)
    return pl.BlockSpec(shape, lambda *_: (0,) * nd, pipeline_mode=pl.Buffered(1))


def _row_spec(rows, width):
    return pl.BlockSpec((rows, width), lambda i: (i, 0))


def _slab_scratch(n):
    return [pltpu.VMEM((ROW_TILE, LANES), F32) for _ in range(n)]


def _emit_qkv(h, w_ref, out_refs, slabs):
    slab = n_mid = 0
    for idx, ((gi, t), o_ref) in enumerate(zip(QKV_ORDER, out_refs)):
        dil = ATTN_DILATIONS[gi]
        blk = jnp.dot(h, w_ref[:, idx * ATTN_OUT:(idx + 1) * ATTN_OUT], preferred_element_type=F32)
        if t == 0:
            blk = blk * (HEAD_DIM ** -0.5)
        if dil == 1:
            o_ref[...] = blk.astype(BF16)
            continue
        rows = ROW_TILE // dil
        for half in range(ATTN_OUT // LANES):
            src = slabs[slab]
            src[...] = blk[:, half * LANES:(half + 1) * LANES]
            slab += 1
            if dil == STRIDE_STEP * STRIDE_STEP:
                mid = slabs[N_QKV_SLABS - 1 - n_mid % 2]
                n_mid += 1
                sub = ROW_TILE // STRIDE_STEP
                for r1 in range(STRIDE_STEP):
                    mid[r1 * sub:(r1 + 1) * sub, :] = src[pl.ds(r1, sub, stride=STRIDE_STEP), :]
                for r in range(dil):
                    r2, r1 = divmod(r, STRIDE_STEP)
                    c0 = r * ATTN_OUT + half * LANES
                    o_ref[:, c0:c0 + LANES] = mid[pl.ds(r1 * sub + r2, rows, stride=STRIDE_STEP), :].astype(BF16)
            else:
                for r in range(dil):
                    c0 = r * ATTN_OUT + half * LANES
                    o_ref[:, c0:c0 + LANES] = src[pl.ds(r, rows, stride=dil), :].astype(BF16)


def _qkv_out(T):
    dils = [ATTN_DILATIONS[gi] for gi, _ in QKV_ORDER]
    specs = [_row_spec(ROW_TILE // d, d * ATTN_OUT) for d in dils]
    shapes = [jax.ShapeDtypeStruct((T // d, d * ATTN_OUT), BF16) for d in dils]
    return specs, shapes


def _proj_kernel(x_ref, g_ref, w_ref, h_ref, *refs):
    h = _rmsnorm(x_ref[...], g_ref[...]).astype(BF16)
    h_ref[...] = h
    _emit_qkv(h, w_ref, refs[:len(QKV_ORDER)], refs[len(QKV_ORDER):])


def _proj(x2d, g, w_qkv):
    T = x2d.shape[0]
    qkv_specs, qkv_shapes = _qkv_out(T)
    return pl.pallas_call(
        _proj_kernel,
        grid=(T // ROW_TILE,),
        in_specs=[_row_spec(ROW_TILE, D_MODEL), _const_spec((1, D_MODEL)), _const_spec(w_qkv.shape)],
        out_specs=[_row_spec(ROW_TILE, D_MODEL)] + qkv_specs,
        out_shape=[jax.ShapeDtypeStruct((T, D_MODEL), BF16)] + qkv_shapes,
        scratch_shapes=_slab_scratch(N_QKV_SLABS),
        compiler_params=pltpu.CompilerParams(dimension_semantics=("arbitrary",),
                                             vmem_limit_bytes=VMEM_LIMIT_BYTES),
        name="proj",
    )(x2d, g, w_qkv)


def _attn_kernel(q_ref, kc_ref, kp_ref, kn_ref, vc_ref, vp_ref, vn_ref, bias_ref, o_ref, l_ref,
                 *, dil, rows, n_tiles):
    i = pl.program_id(1)
    nb = rows // Q_BLOCK
    lane = lax.broadcasted_iota(jnp.int32, (Q_BLOCK, ATTN_OUT), 1)
    head_masks = [(lane >= HEAD_DIM * h) & (lane < HEAD_DIM * (h + 1)) for h in range(HEADS_PER_GROUP)]
    col = lax.broadcasted_iota(jnp.int32, (HEADS_PER_GROUP * Q_BLOCK, K_BLOCK), 1)
    lo_cut = jnp.where(i == 0, ATTN_HALF, 0)
    hi_cut = jnp.where(i == n_tiles - 1, K_BLOCK - ATTN_HALF, K_BLOCK)

    def window(c_ref, p_ref, n_ref, j, cs):
        lo, hi = j * Q_BLOCK - ATTN_HALF, (j + 1) * Q_BLOCK + ATTN_HALF
        parts = []
        if lo < 0:
            parts.append(p_ref[0, :, cs])
            lo = 0
        parts.append(c_ref[0, lo:min(hi, rows), cs])
        if hi > rows:
            parts.append(n_ref[0, :, cs])
        return parts[0] if len(parts) == 1 else jnp.concatenate(parts, axis=0)

    for r in range(dil):
        cs = slice(r * ATTN_OUT, (r + 1) * ATTN_OUT)
        for j in range(nb):
            rs = slice(j * Q_BLOCK, (j + 1) * Q_BLOCK)
            qb = q_ref[0, rs, cs]
            kk = window(kc_ref, kp_ref, kn_ref, j, cs)
            vv = window(vc_ref, vp_ref, vn_ref, j, cs)
            zero = jnp.zeros_like(qb)
            qs = jnp.concatenate([jnp.where(head_masks[h], qb, zero) for h in range(HEADS_PER_GROUP)], axis=0)
            s = lax.dot_general(qs, kk, (((1,), (1,)), ((), ())), preferred_element_type=F32)
            s = s + bias_ref[...]
            if j == 0:
                s = jnp.where(col < lo_cut, NEG_INF, s)
            if j == nb - 1:
                s = jnp.where(col >= hi_cut, NEG_INF, s)
            m = jnp.max(s, axis=-1, keepdims=True)
            p = jnp.exp(s - m)
            den = jnp.sum(p, axis=-1, keepdims=True)
            o = jnp.dot(p.astype(BF16), vv, preferred_element_type=F32)
            o = o * (1.0 / den)
            lse = jnp.broadcast_to(m + jnp.log(den), o.shape)
            out = jnp.zeros((Q_BLOCK, ATTN_OUT), F32)
            lout = jnp.zeros((Q_BLOCK, ATTN_OUT), F32)
            for h in range(HEADS_PER_GROUP):
                hs = slice(h * Q_BLOCK, (h + 1) * Q_BLOCK)
                out = jnp.where(head_masks[h], o[hs], out)
                lout = jnp.where(head_masks[h], lse[hs], lout)
            o_ref[0, rs, cs] = out.astype(BF16)
            l_ref[0, rs, cs] = lout


def _attn(q, k, v, bias, B, S, dil):
    L = S // dil
    width = dil * ATTN_OUT
    rows = ATTN_TILE // dil
    n_tiles = L // rows
    halo_per_tile = rows // ATTN_HALF
    n_halo = L // ATTN_HALF
    qv, kv, vv = (t.reshape(B, L, width) for t in (q, k, v))
    cur = pl.BlockSpec((1, rows, width), lambda b, i: (b, i, 0))
    prev = pl.BlockSpec((1, ATTN_HALF, width), lambda b, i: (b, jnp.maximum(i * halo_per_tile - 1, 0), 0))
    nxt = pl.BlockSpec((1, ATTN_HALF, width),
                       lambda b, i: (b, jnp.minimum((i + 1) * halo_per_tile, n_halo - 1), 0))
    o, l = pl.pallas_call(
        functools.partial(_attn_kernel, dil=dil, rows=rows, n_tiles=n_tiles),
        grid=(B, n_tiles),
        in_specs=[cur, cur, prev, nxt, cur, prev, nxt, _const_spec(bias.shape)],
        out_specs=[cur, cur],
        out_shape=[jax.ShapeDtypeStruct((B, L, width), BF16), jax.ShapeDtypeStruct((B, L, width), F32)],
        compiler_params=pltpu.CompilerParams(dimension_semantics=("arbitrary", "arbitrary"),
                                             vmem_limit_bytes=VMEM_LIMIT_BYTES),
        name=f"attn_d{dil}",
    )(qv, kv, kv, kv, vv, vv, vv, bias)
    return o.reshape(B * L, width), l.reshape(B * L, width)


def _mix_kernel(x_ref, h_ref, hp_ref, hn_ref, o0_ref, o1_ref, o2_ref, l0_ref, l1_ref, l2_ref,
                w_in_ref, pool_w_ref, pool_scale_ref, ln_g_ref, ln_b_ref, sgu_w_ref, sgu_b_ref,
                w_a_ref, w_b_ref, w_c_ref, w_out_ref, g_next_ref, *refs, seq_len, final):
    if final:
        (y_ref, xa_scr, *slabs), w_qkv_ref = refs, None
    else:
        w_qkv_ref, y_ref, h_next_ref, *rest = refs
        qkv_refs, xa_scr, slabs = rest[:len(QKV_ORDER)], rest[len(QKV_ORDER)], rest[len(QKV_ORDER) + 1:]
    TM = ROW_TILE
    tiles_per_seq = seq_len // TM
    pos0 = (pl.program_id(0) % tiles_per_seq) * TM
    h = h_ref[...]

    def zcols(a, b):
        return jnp.dot(h, w_in_ref[:, a:b], preferred_element_type=F32)


    h_ext = jnp.concatenate([hp_ref[...], h, hn_ref[...]], axis=0)
    xa_ext = jnp.dot(h_ext, w_in_ref[:, C_XA:C_GA], preferred_element_type=F32)
    v = zcols(C_V, C_GB)
    u = zcols(C_U, C_V)
    row = lax.broadcasted_iota(jnp.int32, xa_ext.shape, 0) + (pos0 - POOL_HALO)
    xa_ext = jnp.where((row >= 0) & (row < seq_len), xa_ext, 0.0)
    xa_scr[...] = xa_ext

    def token_order(ref, gi, half, slab):
        dil = ATTN_DILATIONS[gi]
        if dil == 1:
            return ref[:, half * LANES:(half + 1) * LANES].astype(F32)
        for r in range(dil):
            c0 = r * ATTN_OUT + half * LANES
            slabs[slab][pl.ds(r, TM // dil, stride=dil), :] = ref[:, c0:c0 + LANES].astype(F32)
        return slabs[slab][...]

    c_halves = []
    slab = 0
    for half in range(ATTN_OUT // LANES):
        os_, ls_ = [], []
        for gi, (o_ref, l_ref) in enumerate(((o0_ref, l0_ref), (o1_ref, l1_ref), (o2_ref, l2_ref))):
            os_.append(token_order(o_ref, gi, half, slab))
            ls_.append(token_order(l_ref, gi, half, slab + 1))
            if ATTN_DILATIONS[gi] > 1:
                slab += 2
        lmax = jnp.maximum(jnp.maximum(ls_[0], ls_[1]), ls_[2])
        es = [jnp.exp(l - lmax) for l in ls_]
        c_halves.append((es[0] * os_[0] + es[1] * os_[1] + es[2] * os_[2]) / (es[0] + es[1] + es[2]))
    c = jnp.concatenate(c_halves, axis=1)

    gate_a = _sigmoid(zcols(C_MG, C_MG + D_MODEL))

    t = lax.broadcasted_iota(jnp.int32, (TM, GROUP_WIDTH), 0) + pos0
    mixed = []
    for gi, w in enumerate(POOL_WINDOWS):
        gs = slice(gi * GROUP_WIDTH, (gi + 1) * GROUP_WIDTH)
        acc = xa_scr[pl.ds(POOL_HALO - w // 2, TM), gs]
        for off in range(-w // 2 + 1, w // 2):
            acc = acc + xa_scr[pl.ds(POOL_HALO + off, TM), gs]
        cnt = (jnp.minimum(t + (w // 2 - 1), seq_len - 1) - jnp.maximum(t - w // 2, 0) + 1).astype(F32)
        mixed.append((acc / cnt - xa_scr[pl.ds(POOL_HALO, TM), gs]).astype(BF16))

    mu = jnp.mean(v, axis=-1, keepdims=True)
    vc = v - mu
    var = jnp.mean(vc * vc, axis=-1, keepdims=True)
    vn = (vc * lax.rsqrt(var + EPS) * ln_g_ref[...] + ln_b_ref[...]).astype(BF16)
    silu_ga = _silu(zcols(C_GA, C_U))
    silu_gb = _silu(zcols(C_GB, C_GC))

    a_mix = jnp.concatenate([jnp.dot(jnp.concatenate(mixed[2 * i:2 * i + 2], axis=1), pool_w_ref[i],
                                     preferred_element_type=F32)
                             for i in range(N_GROUPS // 2)], axis=1) * pool_scale_ref[...]
    a_out = (a_mix * silu_ga).astype(BF16)
    gate_b = _sigmoid(zcols(C_MG + D_MODEL, C_MG + 2 * D_MODEL))

    n_chunks = TM // SGU_CHUNK
    sp_cols = []
    for gi in range(N_GROUPS):
        gs = slice(gi * GROUP_WIDTH, (gi + 1) * GROUP_WIDTH)
        rhs = jnp.concatenate([vn[c_ * SGU_CHUNK:(c_ + 1) * SGU_CHUNK, gs] for c_ in range(n_chunks)], axis=1)
        sp_g = jnp.dot(sgu_w_ref[gi], rhs, preferred_element_type=F32)
        sp_cols.append(jnp.concatenate(
            [sp_g[:, c_ * GROUP_WIDTH:(c_ + 1) * GROUP_WIDTH] + sgu_b_ref[gi] for c_ in range(n_chunks)], axis=0))
    sp = jnp.concatenate(sp_cols, axis=1)
    merged = gate_a * jnp.dot(a_out, w_a_ref[...], preferred_element_type=F32)
    b_out = (u * sp * silu_gb).astype(BF16)
    silu_gc = _silu(zcols(C_GC, C_MG))
    gate_c = _sigmoid(zcols(C_MG + 2 * D_MODEL, C_END))
    merged = merged + gate_b * jnp.dot(b_out, w_b_ref[...], preferred_element_type=F32)

    c_out = (c * silu_gc).astype(BF16)
    merged = merged + gate_c * jnp.dot(c_out, w_c_ref[...], preferred_element_type=F32)

    y = x_ref[...] + jnp.dot(merged.astype(BF16), w_out_ref[...], preferred_element_type=F32)
    if final:
        y_ref[...] = _rmsnorm(y, g_next_ref[...])
        return
    y_ref[...] = y
    h_next = _rmsnorm(y, g_next_ref[...]).astype(BF16)
    h_next_ref[...] = h_next
    _emit_qkv(h_next, w_qkv_ref, qkv_refs, slabs)


def _mix(x2d, h2d, seq_len, attn_o, attn_l, lw, g_next, w_qkv_next):
    final = w_qkv_next is None
    T = x2d.shape[0]
    halo_per_tile = ROW_TILE // POOL_HALO
    n_halo = T // POOL_HALO
    n_il_slabs = 2 * (ATTN_OUT // LANES) * sum(d > 1 for d in ATTN_DILATIONS)
    prev = pl.BlockSpec((POOL_HALO, D_MODEL), lambda i: (jnp.maximum(i * halo_per_tile - 1, 0), 0))
    nxt = pl.BlockSpec((POOL_HALO, D_MODEL), lambda i: (jnp.minimum((i + 1) * halo_per_tile, n_halo - 1), 0))
    weights = [lw["w_in"], lw["pool_w"], lw["pool_scale"], lw["ln_g"], lw["ln_b"],
               lw["sgu_w"], lw["sgu_b"], lw["w_a"], lw["w_b"], lw["w_c"], lw["w_out"], g_next]
    out_specs = [_row_spec(ROW_TILE, D_MODEL)]
    out_shape = [jax.ShapeDtypeStruct((T, D_MODEL), F32)]
    if not final:
        weights.append(w_qkv_next)
        qkv_specs, qkv_shapes = _qkv_out(T)
        out_specs += [_row_spec(ROW_TILE, D_MODEL)] + qkv_specs
        out_shape += [jax.ShapeDtypeStruct((T, D_MODEL), BF16)] + qkv_shapes
        n_il_slabs = max(n_il_slabs, N_QKV_SLABS)
    scratch = [pltpu.VMEM((ROW_TILE + 2 * POOL_HALO, N_GROUPS * GROUP_WIDTH), F32)] + _slab_scratch(n_il_slabs)
    outs = pl.pallas_call(
        functools.partial(_mix_kernel, seq_len=seq_len, final=final),
        grid=(T // ROW_TILE,),
        in_specs=[_row_spec(ROW_TILE, D_MODEL), _row_spec(ROW_TILE, D_MODEL), prev, nxt]
        + 2 * [_row_spec(ROW_TILE // d, d * ATTN_OUT) for d in ATTN_DILATIONS]
        + [_const_spec(w.shape) for w in weights],
        out_specs=out_specs,
        out_shape=out_shape,
        scratch_shapes=scratch,
        compiler_params=pltpu.CompilerParams(dimension_semantics=("arbitrary",),
                                             vmem_limit_bytes=VMEM_LIMIT_BYTES),
        name="mix_final" if final else "mix",
    )(x2d, h2d, h2d, h2d, *attn_o, *attn_l, *weights)
    return outs


def _t5_bucket(rel):
    half = N_BUCKETS // 2
    n = -rel
    ret = (n < 0).astype(np.int32) * half
    n = np.abs(n)
    max_exact = half // 2
    large = max_exact + (np.log(np.maximum(n, 1) / max_exact) / np.log(T5_MAX_DIST / max_exact)
                         * (half - max_exact)).astype(np.int32)
    large = np.minimum(large, half - 1)
    return (ret + np.where(n < max_exact, n, large)).astype(np.int32)


def _attn_bias(rel_bias, gi, dil):
    rel = np.arange(K_BLOCK)[None, :] - ATTN_HALF - np.arange(Q_BLOCK)[:, None]
    table = rel_bias[:, gi * HEADS_PER_GROUP:(gi + 1) * HEADS_PER_GROUP].astype(F32)
    onehot = (_t5_bucket(rel * dil)[..., None] == np.arange(N_BUCKETS)).astype(np.float32)
    bias = jnp.einsum("qkb,bh->hqk", onehot, table, precision=lax.Precision.HIGHEST)
    bias = jnp.where((np.abs(rel) <= ATTN_HALF)[None], bias, NEG_INF)
    return bias.reshape(HEADS_PER_GROUP * Q_BLOCK, K_BLOCK)


def _pair_block_diag(w):
    G, C, _ = w.shape
    z = jnp.zeros((G // 2, C, C), w.dtype)
    top = jnp.concatenate([w[0::2], z], axis=2)
    bot = jnp.concatenate([z, w[1::2]], axis=2)
    return jnp.concatenate([top, bot], axis=1)


def _qkv_weights(w_in_l):
    cols = []
    for gi, t in QKV_ORDER:
        c0 = R_QKV + t * len(ATTN_DILATIONS) * ATTN_OUT + gi * ATTN_OUT
        cols.append(w_in_l[:, c0:c0 + ATTN_OUT])
    return jnp.concatenate(cols, axis=1).astype(BF16)


def _layer_weights(l, norm_g, w_in, pool_w, pool_scale, sgu_ln_g, sgu_ln_b, sgu_w, sgu_b,
                   w_br_a, w_br_b, w_br_c, w_out):
    return {
        "norm_g": norm_g[l].reshape(1, D_MODEL),
        "w_in": jnp.concatenate([w_in[l, :, :R_QKV], w_in[l, :, R_GC:]], axis=1).astype(BF16),
        "w_qkv": _qkv_weights(w_in[l]),
        "pool_w": _pair_block_diag(pool_w[l].astype(BF16)),
        "pool_scale": pool_scale[l].reshape(1, -1),
        "ln_g": sgu_ln_g[l].reshape(1, -1),
        "ln_b": sgu_ln_b[l].reshape(1, -1),
        "sgu_w": sgu_w[l].astype(BF16),
        "sgu_b": jnp.broadcast_to(sgu_b[l][:, :, None], (N_GROUPS, SGU_CHUNK, GROUP_WIDTH)),
        "w_a": w_br_a[l].astype(BF16),
        "w_b": w_br_b[l].astype(BF16),
        "w_c": w_br_c[l].astype(BF16),
        "w_out": w_out[l].astype(BF16),
    }


def _encoder(x, layers, biases, final_g):
    B, S, _ = x.shape
    assert S % ATTN_TILE == 0 and S % (ATTN_TILE // max(ATTN_DILATIONS)) == 0
    x2d = x.reshape(B * S, D_MODEL)
    h2d, *qkv = _proj(x2d, layers[0]["norm_g"], layers[0]["w_qkv"])
    for l, lw in enumerate(layers):
        by_group = {key: arr for key, arr in zip(QKV_ORDER, qkv)}
        outs = [_attn(by_group[gi, 0], by_group[gi, 1], by_group[gi, 2], biases[gi], B, S, dil)
                for gi, dil in enumerate(ATTN_DILATIONS)]
        attn_o, attn_l = [o for o, _ in outs], [lse for _, lse in outs]
        if l + 1 < len(layers):
            x2d, h2d, *qkv = _mix(x2d, h2d, S, attn_o, attn_l, lw,
                                  layers[l + 1]["norm_g"], layers[l + 1]["w_qkv"])
        else:
            (x2d,) = _mix(x2d, h2d, S, attn_o, attn_l, lw, final_g, None)
    return x2d.reshape(B, S, D_MODEL)


def kernel(x_prompt, x_sample, norm_g, w_in, pool_w, pool_scale, sgu_ln_g, sgu_ln_b, sgu_w, sgu_b,
           rel_bias, w_br_a, w_br_b, w_br_c, w_out, final_g):
    layers = [_layer_weights(l, norm_g, w_in, pool_w, pool_scale, sgu_ln_g, sgu_ln_b, sgu_w, sgu_b,
                             w_br_a, w_br_b, w_br_c, w_out) for l in range(DEPTH)]
    biases = [_attn_bias(rel_bias, gi, dil) for gi, dil in enumerate(ATTN_DILATIONS)]
    fg = final_g.reshape(1, D_MODEL)
    return (_encoder(x_prompt, layers, biases, fg), _encoder(x_sample, layers, biases, fg))
```
